```python
import jax, jax.numpy as jnp
from jax import lax
import numpy as np


D_MODEL = 2048
BATCH = 4
SEQ = 4096
DEPTH = 4

CHUNK = 64
N_MIXERS = 2
N_POOL_LAYERS = (DEPTH + N_MIXERS - 1) // N_MIXERS
N_LRU_LAYERS = DEPTH // N_MIXERS
POOL_WINDOWS = (2, 4, 8, 16)
POOL_GROUPS = 4
POOL_GROUP_DIM = D_MODEL // POOL_GROUPS
LRU_WIDTH = D_MODEL
LRU_HEADS = 16
LRU_HEAD_DIM = LRU_WIDTH // LRU_HEADS
CONV_WIDTH = 4
LRU_C = 8.0
FFN_DIM = 256 * ((8 * D_MODEL // 3 + 255) // 256)
MEM_LEN = 256
XATTN_HEADS = 4
XATTN_HEAD_DIM = D_MODEL // XATTN_HEADS
MACARON_WEIGHT = 0.5
EPS = 1e-6

kernel_name = 'hybrid_pool_rglru_macaron_memxattn'


def rmsnorm(x, g):
    xf = x.astype(jnp.float32)
    y = xf * lax.rsqrt(jnp.mean(xf * xf, axis=-1, keepdims=True) + EPS)
    return (y * g.astype(jnp.float32)).astype(x.dtype)


def swiglu(u, w_gate, w_up, w_down):
    return (jax.nn.silu(u @ w_gate) * (u @ w_up)) @ w_down


def pool_mixer(u, w_group, scale):
    b, s, d = u.shape
    uf = u.astype(jnp.float32).reshape(b, s, POOL_GROUPS, POOL_GROUP_DIM)
    cs = jnp.concatenate([jnp.zeros((b, 1, POOL_GROUPS, POOL_GROUP_DIM), jnp.float32),
                          jnp.cumsum(uf, axis=1)], axis=1)
    pos = jnp.arange(1, s + 1, dtype=jnp.float32)[None, :, None]
    outs = []
    for g, w in enumerate(POOL_WINDOWS):
        c = cs[:, :, g]
        lower = jnp.concatenate([jnp.zeros((b, w - 1, POOL_GROUP_DIM), jnp.float32),
                                 c[:, :s + 1 - w]], axis=1)
        count = jnp.minimum(pos, float(w))
        outs.append((c[:, 1:] - lower) / count - uf[:, :, g])
    pooled = jnp.stack(outs, axis=2).astype(u.dtype)
    y = jnp.einsum('bsgi,gij->bsgj', pooled, w_group).reshape(b, s, d)
    return y * scale


def rglru_block(u, w_in, conv_w, conv_b, w_a, b_a, w_x, b_x, lam, w_out):
    b, s, _ = u.shape
    proj = u @ w_in
    gate, xr = proj[..., :LRU_WIDTH], proj[..., LRU_WIDTH:]
    xp = jnp.pad(xr, ((0, 0), (CONV_WIDTH - 1, 0), (0, 0)))
    xc = conv_b
    for k in range(CONV_WIDTH):
        xc = xc + xp[:, k:k + s] * conv_w[k]
    xh = xc.reshape(b, s, LRU_HEADS, LRU_HEAD_DIM)
    r = jax.nn.sigmoid((jnp.einsum('bshi,hij->bshj', xh, w_a).reshape(b, s, LRU_WIDTH) + b_a).astype(jnp.float32))
    ig = jax.nn.sigmoid((jnp.einsum('bshi,hij->bshj', xh, w_x).reshape(b, s, LRU_WIDTH) + b_x).astype(jnp.float32))
    log_a = -LRU_C * r * jax.nn.softplus(-lam.astype(jnp.float32))
    a = jnp.exp(log_a)
    bterm = jnp.sqrt(-jnp.expm1(2.0 * log_a)) * ig * xc.astype(jnp.float32)

    def combine(lhs, rhs):
        a1, b1 = lhs
        a2, b2 = rhs
        return a1 * a2, a2 * b1 + b2

    _, h = lax.associative_scan(combine, (a, bterm), axis=1)
    y = h.astype(u.dtype) * jax.nn.gelu(gate)
    return y @ w_out


def mem_cross_attention(u, m, w_q, w_k, w_v, w_o):
    b, s, d = u.shape
    ml = m.shape[1]
    q = (u @ w_q).reshape(b, s, XATTN_HEADS, XATTN_HEAD_DIM)
    k = (m @ w_k).reshape(b, ml, XATTN_HEADS, XATTN_HEAD_DIM)
    v = (m @ w_v).reshape(b, ml, XATTN_HEADS, XATTN_HEAD_DIM)
    scores = jnp.einsum('bshd,bmhd->bhsm', q, k).astype(jnp.float32) * (XATTN_HEAD_DIM ** -0.5)
    p = jax.nn.softmax(scores, axis=-1).astype(v.dtype)
    o = jnp.einsum('bhsm,bmhd->bshd', p, v).reshape(b, s, d)
    return o @ w_o


def setup_inputs(seed: int = 0) -> dict:
    key = jax.random.key(seed)
    ks = jax.random.split(key, 26)
    f32 = jnp.float32

    def nrm(k, shape, fan_in):
        return jax.random.normal(k, shape, f32) * (fan_in ** -0.5)

    def gain(k, shape):
        return 1.0 + 0.02 * jax.random.normal(k, shape, f32)

    a8 = jax.random.uniform(ks[16], (N_LRU_LAYERS, LRU_WIDTH), f32, 0.9, 0.999)
    s_lam = a8 ** (1.0 / LRU_C)
    lru_lambda = jnp.log(s_lam) - jnp.log1p(-s_lam)
    return {
        'x': jax.random.normal(ks[0], (BATCH, SEQ, D_MODEL), f32),
        'mem': jax.random.normal(ks[1], (BATCH, MEM_LEN, D_MODEL), f32),
        'ffn_norm': gain(ks[2], (DEPTH, 2, D_MODEL)),
        'w_ffn_gate': nrm(ks[3], (DEPTH, 2, D_MODEL, FFN_DIM), D_MODEL),
        'w_ffn_up': nrm(ks[4], (DEPTH, 2, D_MODEL, FFN_DIM), D_MODEL),
        'w_ffn_down': nrm(ks[5], (DEPTH, 2, FFN_DIM, D_MODEL), FFN_DIM),
        'mix_norm': gain(ks[6], (DEPTH, D_MODEL)),
        'pool_w': nrm(ks[7], (N_POOL_LAYERS, POOL_GROUPS, POOL_GROUP_DIM, POOL_GROUP_DIM), POOL_GROUP_DIM),
        'pool_scale': gain(ks[8], (N_POOL_LAYERS, D_MODEL)),
        'lru_w_in': nrm(ks[9], (N_LRU_LAYERS, D_MODEL, 2 * LRU_WIDTH), D_MODEL),
        'lru_conv_w': nrm(ks[10], (N_LRU_LAYERS, CONV_WIDTH, LRU_WIDTH), CONV_WIDTH),
        'lru_conv_b': 0.01 * jax.random.normal(ks[11], (N_LRU_LAYERS, LRU_WIDTH), f32),
        'lru_w_a': nrm(ks[12], (N_LRU_LAYERS, LRU_HEADS, LRU_HEAD_DIM, LRU_HEAD_DIM), LRU_HEAD_DIM),
        'lru_b_a': 0.01 * jax.random.normal(ks[13], (N_LRU_LAYERS, LRU_WIDTH), f32),
        'lru_w_x': nrm(ks[14], (N_LRU_LAYERS, LRU_HEADS, LRU_HEAD_DIM, LRU_HEAD_DIM), LRU_HEAD_DIM),
        'lru_b_x': 0.01 * jax.random.normal(ks[15], (N_LRU_LAYERS, LRU_WIDTH), f32),
        'lru_lambda': lru_lambda,
        'lru_w_out': nrm(ks[17], (N_LRU_LAYERS, LRU_WIDTH, D_MODEL), LRU_WIDTH),
        'xattn_norm': gain(ks[18], (DEPTH, D_MODEL)),
        'mem_norm': gain(ks[19], (D_MODEL,)),
        'w_q': nrm(ks[20], (DEPTH, D_MODEL, D_MODEL), D_MODEL),
        'w_k': nrm(ks[21], (DEPTH, D_MODEL, D_MODEL), D_MODEL),
        'w_v': nrm(ks[22], (DEPTH, D_MODEL, D_MODEL), D_MODEL),
        'w_o': nrm(ks[23], (DEPTH, D_MODEL, D_MODEL), D_MODEL),
        'final_norm': gain(ks[24], (D_MODEL,)),
    }


def reference(x, mem, ffn_norm, w_ffn_gate, w_ffn_up, w_ffn_down, mix_norm, pool_w, pool_scale,
              lru_w_in, lru_conv_w, lru_conv_b, lru_w_a, lru_b_a, lru_w_x, lru_b_x, lru_lambda,
              lru_w_out, xattn_norm, mem_norm, w_q, w_k, w_v, w_o, final_norm):
    m = rmsnorm(mem, mem_norm)
    h = x
    for i in range(DEPTH):
        h = h + MACARON_WEIGHT * swiglu(rmsnorm(h, ffn_norm[i, 0]), w_ffn_gate[i, 0], w_ffn_up[i, 0], w_ffn_down[i, 0])
        u = rmsnorm(h, mix_norm[i])
        j = i // N_MIXERS
        if i % N_MIXERS == 0:
            h = h + pool_mixer(u, pool_w[j], pool_scale[j])
        else:
            h = h + rglru_block(u, lru_w_in[j], lru_conv_w[j], lru_conv_b[j], lru_w_a[j], lru_b_a[j],
                                lru_w_x[j], lru_b_x[j], lru_lambda[j], lru_w_out[j])
        h = h + mem_cross_attention(rmsnorm(h, xattn_norm[i]), m, w_q[i], w_k[i], w_v[i], w_o[i])
        h = h + MACARON_WEIGHT * swiglu(rmsnorm(h, ffn_norm[i, 1]), w_ffn_gate[i, 1], w_ffn_up[i, 1], w_ffn_down[i, 1])
    return rmsnorm(h, final_norm)
```

```python
import functools

import jax
import jax.numpy as jnp
from jax import lax
from jax.experimental import pallas as pl
from jax.experimental.pallas import tpu as pltpu

F32 = jnp.float32
BF16 = jnp.bfloat16

EPS = 1e-6
MACARON_WEIGHT = 0.5
POOL_WINDOWS = (2, 4, 8, 16)
LRU_HEADS = 16
LRU_C = 8.0
XATTN_HEADS = 4

SUBLANES_F32 = 8
SUBLANES_BF16 = 16
VMEM_LIMIT_BYTES = 58 * 1024 * 1024

FFN_TOKEN_BLOCK = 512
FFN_HIDDEN_BLOCK = 512
POOL_TOKEN_BLOCK = 512
LRU_TOKEN_BLOCK = 256
XATTN_TOKEN_BLOCK = 512
KV_COLUMN_BLOCK = 1024


def _rmsnorm(x, g):
    ms = jnp.mean(x * x, axis=-1, keepdims=True)
    return x * lax.rsqrt(ms + EPS) * g


def _params(semantics):
    return pltpu.CompilerParams(dimension_semantics=semantics, vmem_limit_bytes=VMEM_LIMIT_BYTES)


def _resident(block_shape, index_map):
    return pl.BlockSpec(block_shape, index_map, pipeline_mode=pl.Buffered(1))


def _ffn_kernel(h_ref, g_ref, wg_ref, wu_ref, wd_ref, fg_ref, o_ref, u_ref, *, apply_final_norm):
    j = pl.program_id(1)

    @pl.when(j == 0)
    def _():
        x = h_ref[...]
        u_ref[...] = _rmsnorm(x, g_ref[...]).astype(BF16)
        o_ref[...] = x

    u = u_ref[...]
    gate = jnp.dot(u, wg_ref[...], preferred_element_type=F32)
    up = jnp.dot(u, wu_ref[...], preferred_element_type=F32)
    act = (jax.nn.silu(gate) * up * MACARON_WEIGHT).astype(BF16)
    o_ref[...] += jnp.dot(act, wd_ref[...], preferred_element_type=F32)

    if apply_final_norm:

        @pl.when(j == pl.num_programs(1) - 1)
        def _():
            o_ref[...] = _rmsnorm(o_ref[...], fg_ref[...])


def _ffn(h, norms, wg, wu, wd, final_g, layer, half, apply_final_norm):
    m, d = h.shape
    f = wg.shape[-1]
    tm, tf = FFN_TOKEN_BLOCK, FFN_HIDDEN_BLOCK
    norm_row = layer * 2 + half
    return pl.pallas_call(
        functools.partial(_ffn_kernel, apply_final_norm=apply_final_norm),
        grid=(m // tm, f // tf),
        in_specs=[
            pl.BlockSpec((tm, d), lambda i, j: (i, 0)),
            pl.BlockSpec((None, 1, d), lambda i, j: (norm_row, 0, 0)),
            pl.BlockSpec((None, None, d, tf), lambda i, j: (layer, half, 0, j)),
            pl.BlockSpec((None, None, d, tf), lambda i, j: (layer, half, 0, j)),
            pl.BlockSpec((None, None, tf, d), lambda i, j: (layer, half, j, 0)),
            pl.BlockSpec((1, d), lambda i, j: (0, 0)),
        ],
        out_specs=pl.BlockSpec((tm, d), lambda i, j: (i, 0)),
        out_shape=jax.ShapeDtypeStruct((m, d), F32),
        scratch_shapes=[pltpu.VMEM((tm, d), BF16)],
        compiler_params=_params(("parallel", "arbitrary")),
        name=f"ffn_l{layer}_{half}",
    )(h, norms, wg, wu, wd, final_g)


def _pool_kernel(h_ref, g_ref, w_ref, sc_ref, o_ref, buf_ref, *, blocks_per_seq, halo):
    tm = h_ref.shape[0]
    groups = len(POOL_WINDOWS)
    gd = h_ref.shape[1] // groups
    blk = pl.program_id(0) % blocks_per_seq

    @pl.when(blk == 0)
    def _():
        buf_ref[0:halo, :] = jnp.zeros((halo, buf_ref.shape[1]), F32)

    x = h_ref[...]
    buf_ref[halo:, :] = _rmsnorm(x, g_ref[...])
    pos = (blk * tm + 1 + lax.broadcasted_iota(jnp.int32, (tm, 1), 0)).astype(F32)

    for g, w in enumerate(POOL_WINDOWS):
        cols = slice(g * gd, (g + 1) * gd)
        cur = buf_ref[halo:halo + tm, cols]
        acc = cur
        for k in range(1, w):
            acc = acc + buf_ref[halo - k:halo - k + tm, cols]
        count = jnp.minimum(pos, float(w))
        pooled = (acc / count - cur).astype(BF16)
        y = jnp.dot(pooled, w_ref[g], preferred_element_type=F32)
        o_ref[:, cols] = x[:, cols] + y * sc_ref[:, cols]

    buf_ref[0:halo, :] = buf_ref[tm:tm + halo, :]


def _pool(h, norms, pool_w, pool_scale, layer, j, seq_len):
    m, d = h.shape
    tm = POOL_TOKEN_BLOCK
    halo = max(POOL_WINDOWS)
    groups, gd = pool_w.shape[1], pool_w.shape[2]
    return pl.pallas_call(
        functools.partial(_pool_kernel, blocks_per_seq=seq_len // tm, halo=halo),
        grid=(m // tm,),
        in_specs=[
            pl.BlockSpec((tm, d), lambda i: (i, 0)),
            pl.BlockSpec((None, 1, d), lambda i: (layer, 0, 0)),
            _resident((None, groups, gd, gd), lambda i: (j, 0, 0, 0)),
            pl.BlockSpec((None, 1, d), lambda i: (j, 0, 0)),
        ],
        out_specs=pl.BlockSpec((tm, d), lambda i: (i, 0)),
        out_shape=jax.ShapeDtypeStruct((m, d), F32),
        scratch_shapes=[pltpu.VMEM((halo + tm, d), F32)],
        compiler_params=_params(("arbitrary",)),
        name=f"pool_l{layer}",
    )(h, norms, pool_w, pool_scale)


def _softplus(z):
    return jnp.maximum(z, 0.0) + jnp.log1p(jnp.exp(-jnp.abs(z)))


def _lru_kernel(h_ref, g_ref, win_ref, cw_ref, cb_ref, wax_ref, ba_ref, bx_ref, lam_ref, wout_ref,
                o_ref, proj_ref, a_ref, b_ref, y_ref, carry_ref, *, blocks_per_seq, conv_width):
    tm = h_ref.shape[0]
    r = a_ref.shape[1]
    hd_dim = r // LRU_HEADS
    tail = SUBLANES_F32
    blk = pl.program_id(0) % blocks_per_seq

    @pl.when(blk == 0)
    def _():
        proj_ref[0:tail, r:] = jnp.zeros((tail, r), F32)
        carry_ref[...] = jnp.zeros(carry_ref.shape, F32)

    x = h_ref[...]
    u = _rmsnorm(x, g_ref[...]).astype(BF16)
    proj_ref[tail:, :] = jnp.dot(u, win_ref[...], preferred_element_type=F32)

    decay = _softplus(-lam_ref[...])
    for hd in range(LRU_HEADS):
        cols = slice(hd * hd_dim, (hd + 1) * hd_dim)
        xcols = slice(r + hd * hd_dim, r + (hd + 1) * hd_dim)
        xc = cb_ref[:, cols]
        for k in range(conv_width):
            back = conv_width - 1 - k
            xc = xc + proj_ref[tail - back:tail - back + tm, xcols] * cw_ref[k:k + 1, cols]
        ra = jnp.dot(xc.astype(BF16), wax_ref[hd], preferred_element_type=F32)
        rg = jax.nn.sigmoid(ra[:, :hd_dim] + ba_ref[:, cols])
        ig = jax.nn.sigmoid(ra[:, hd_dim:] + bx_ref[:, cols])
        log_a = (-LRU_C) * rg * decay[:, cols]
        a = jnp.exp(log_a)
        a_ref[:, cols] = a
        b_ref[:, cols] = jnp.sqrt(1.0 - a * a) * ig * xc

    proj_ref[0:tail, r:] = proj_ref[tm:tm + tail, r:]

    rows = lax.broadcasted_iota(jnp.int32, (SUBLANES_F32, r), 0)

    def scan_group(row0, carry):
        a = a_ref[pl.ds(row0, SUBLANES_F32), :]
        b = b_ref[pl.ds(row0, SUBLANES_F32), :]
        for dist in (1, 2, 4):
            keep = rows >= dist
            a_prev = jnp.where(keep, pltpu.roll(a, dist, 0), 1.0)
            b_prev = jnp.where(keep, pltpu.roll(b, dist, 0), 0.0)
            b = a * b_prev + b
            a = a * a_prev
        hh = b + a * carry
        gate = proj_ref[pl.ds(tail + row0, SUBLANES_F32), 0:r]
        y = hh * jax.nn.gelu(gate)
        return y, jnp.broadcast_to(hh[SUBLANES_F32 - 1:, :], (SUBLANES_F32, r))

    def scan_pair(p, carry):
        row0 = pl.multiple_of(p * SUBLANES_BF16, SUBLANES_BF16)
        y0, carry = scan_group(row0, carry)
        y1, carry = scan_group(row0 + SUBLANES_F32, carry)
        y_ref[pl.ds(row0, SUBLANES_BF16), :] = jnp.concatenate([y0, y1], axis=0).astype(BF16)
        return carry

    carry_ref[...] = lax.fori_loop(0, tm // SUBLANES_BF16, scan_pair, carry_ref[...])
    o_ref[...] = x + jnp.dot(y_ref[...], wout_ref[...], preferred_element_type=F32)


def _lru(h, norms, w_in, conv_w, conv_b, w_ax, b_a, b_x, lam, w_out, layer, j, seq_len):
    m, d = h.shape
    tm = LRU_TOKEN_BLOCK
    r = w_out.shape[1]
    conv_width = conv_w.shape[1]
    heads, hd_dim = w_ax.shape[1], w_ax.shape[2]
    vec = lambda idx: pl.BlockSpec((None, 1, r), lambda i: (idx, 0, 0))
    return pl.pallas_call(
        functools.partial(_lru_kernel, blocks_per_seq=seq_len // tm, conv_width=conv_width),
        grid=(m // tm,),
        in_specs=[
            pl.BlockSpec((tm, d), lambda i: (i, 0)),
            pl.BlockSpec((None, 1, d), lambda i: (layer, 0, 0)),
            _resident((None, d, 2 * r), lambda i: (j, 0, 0)),
            pl.BlockSpec((None, conv_width, r), lambda i: (j, 0, 0)),
            vec(j),
            _resident((None, heads, hd_dim, 2 * hd_dim), lambda i: (j, 0, 0, 0)),
            vec(j),
            vec(j),
            vec(j),
            _resident((None, r, d), lambda i: (j, 0, 0)),
        ],
        out_specs=pl.BlockSpec((tm, d), lambda i: (i, 0)),
        out_shape=jax.ShapeDtypeStruct((m, d), F32),
        scratch_shapes=[
            pltpu.VMEM((SUBLANES_F32 + tm, 2 * r), F32),
            pltpu.VMEM((tm, r), F32),
            pltpu.VMEM((tm, r), F32),
            pltpu.VMEM((tm, r), BF16),
            pltpu.VMEM((SUBLANES_F32, r), F32),
        ],
        compiler_params=_params(("arbitrary",)),
        name=f"lru_l{layer}",
    )(h, norms, w_in, conv_w, conv_b, w_ax, b_a, b_x, lam, w_out)


def _kv_kernel(m_ref, g_ref, wk_ref, wv_ref, k_ref, v_ref):
    mn = _rmsnorm(m_ref[...], g_ref[...]).astype(BF16)
    k_ref[...] = jnp.dot(mn, wk_ref[...], preferred_element_type=F32).astype(BF16)
    v_ref[...] = jnp.dot(mn, wv_ref[...], preferred_element_type=F32).astype(BF16)


def _kv(mem2d, mem_norm, w_k, w_v):
    rows, d = mem2d.shape
    depth = w_k.shape[0]
    tn = KV_COLUMN_BLOCK
    out = jax.ShapeDtypeStruct((depth, rows, d), BF16)
    return pl.pallas_call(
        _kv_kernel,
        grid=(depth, d // tn),
        in_specs=[
            _resident((rows, d), lambda l, n: (0, 0)),
            pl.BlockSpec((1, d), lambda l, n: (0, 0)),
            pl.BlockSpec((None, d, tn), lambda l, n: (l, 0, n)),
            pl.BlockSpec((None, d, tn), lambda l, n: (l, 0, n)),
        ],
        out_specs=[pl.BlockSpec((None, rows, tn), lambda l, n: (l, 0, n))] * 2,
        out_shape=[out, out],
        compiler_params=_params(("arbitrary", "arbitrary")),
        name="kv_proj",
    )(mem2d, mem_norm, w_k, w_v)


def _xattn_kernel(h_ref, g_ref, wq_ref, k_ref, v_ref, wo_ref, o_ref, q_ref, a_ref):
    d = h_ref.shape[1]
    hd_dim = d // XATTN_HEADS
    scale = hd_dim ** -0.5
    x = h_ref[...]
    u = _rmsnorm(x, g_ref[...]).astype(BF16)
    q_ref[...] = jnp.dot(u, wq_ref[...], preferred_element_type=F32).astype(BF16)
    for hd in range(XATTN_HEADS):
        cols = slice(hd * hd_dim, (hd + 1) * hd_dim)
        s = lax.dot_general(q_ref[:, cols], k_ref[:, cols], (((1,), (1,)), ((), ())),
                            preferred_element_type=F32) * scale
        e = jnp.exp(s - jnp.max(s, axis=-1, keepdims=True))
        p = (e / jnp.sum(e, axis=-1, keepdims=True)).astype(BF16)
        a_ref[:, cols] = jnp.dot(p, v_ref[:, cols], preferred_element_type=F32).astype(BF16)
    o_ref[...] = x + jnp.dot(a_ref[...], wo_ref[...], preferred_element_type=F32)


def _xattn(h, norms, w_q, k_all, v_all, w_o, layer, seq_len, mem_len):
    m, d = h.shape
    tm = XATTN_TOKEN_BLOCK
    blocks_per_seq = seq_len // tm
    return pl.pallas_call(
        _xattn_kernel,
        grid=(m // tm,),
        in_specs=[
            pl.BlockSpec((tm, d), lambda i: (i, 0)),
            pl.BlockSpec((None, 1, d), lambda i: (layer, 0, 0)),
            _resident((None, d, d), lambda i: (layer, 0, 0)),
            pl.BlockSpec((None, mem_len, d), lambda i: (layer, i // blocks_per_seq, 0)),
            pl.BlockSpec((None, mem_len, d), lambda i: (layer, i // blocks_per_seq, 0)),
            _resident((None, d, d), lambda i: (layer, 0, 0)),
        ],
        out_specs=pl.BlockSpec((tm, d), lambda i: (i, 0)),
        out_shape=jax.ShapeDtypeStruct((m, d), F32),
        scratch_shapes=[pltpu.VMEM((tm, d), BF16), pltpu.VMEM((tm, d), BF16)],
        compiler_params=_params(("arbitrary",)),
        name=f"xattn_l{layer}",
    )(h, norms, w_q, k_all, v_all, w_o)


def kernel(x, mem, ffn_norm, w_ffn_gate, w_ffn_up, w_ffn_down, mix_norm, pool_w, pool_scale, lru_w_in, lru_conv_w, lru_conv_b, lru_w_a, lru_b_a, lru_w_x, lru_b_x, lru_lambda, lru_w_out, xattn_norm, mem_norm, w_q, w_k, w_v, w_o, final_norm):
    batch, seq_len, d = x.shape
    mem_len = mem.shape[1]
    depth = ffn_norm.shape[0]
    n_mixers = 2

    row = lambda a: a.reshape(-1, 1, a.shape[-1])
    ffn_norm_r, mix_norm_r, xattn_norm_r = row(ffn_norm), row(mix_norm), row(xattn_norm)
    pool_scale_r = row(pool_scale)
    conv_b_r, b_a_r, b_x_r, lam_r = row(lru_conv_b), row(lru_b_a), row(lru_b_x), row(lru_lambda)
    final_g = final_norm.reshape(1, d)

    wg, wu, wd = w_ffn_gate.astype(BF16), w_ffn_up.astype(BF16), w_ffn_down.astype(BF16)
    pool_w_b = pool_w.astype(BF16)
    w_in_b, w_out_b = lru_w_in.astype(BF16), lru_w_out.astype(BF16)
    w_ax = jnp.concatenate([lru_w_a, lru_w_x], axis=-1).astype(BF16)
    w_q_b, w_o_b = w_q.astype(BF16), w_o.astype(BF16)

    k_all, v_all = _kv(mem.reshape(batch * mem_len, d), mem_norm.reshape(1, d),
                       w_k.astype(BF16), w_v.astype(BF16))

    h = x.reshape(batch * seq_len, d)
    for i in range(depth):
        j = i // n_mixers
        h = _ffn(h, ffn_norm_r, wg, wu, wd, final_g, i, 0, False)
        if i % n_mixers == 0:
            h = _pool(h, mix_norm_r, pool_w_b, pool_scale_r, i, j, seq_len)
        else:
            h = _lru(h, mix_norm_r, w_in_b, lru_conv_w, conv_b_r, w_ax, b_a_r, b_x_r, lam_r, w_out_b,
                     i, j, seq_len)
        h = _xattn(h, xattn_norm_r, w_q_b, k_all, v_all, w_o_b, i, seq_len, mem_len)
        h = _ffn(h, ffn_norm_r, wg, wu, wd, final_g, i, 1, i == depth - 1)
    return h.reshape(batch, seq_len, d)
```

```python
import functools

import jax
import jax.numpy as jnp
from jax import lax
from jax.experimental import pallas as pl
from jax.experimental.pallas import tpu as pltpu

F32 = jnp.float32
BF16 = jnp.bfloat16

EPS = 1e-6
MACARON_WEIGHT = 0.5
POOL_WINDOWS = (2, 4, 8, 16)
LRU_HEADS = 16
LRU_C = 8.0
XATTN_HEADS = 4

SUBLANES_F32 = 8
SUBLANES_BF16 = 16
VMEM_LIMIT_BYTES = 58 * 1024 * 1024

FFN_TOKEN_BLOCK = 1024
FFN_HIDDEN_BLOCK = 512
POOL_TOKEN_BLOCK = 512
LRU_TOKEN_BLOCK = 256
XATTN_TOKEN_BLOCK = 512
KV_COLUMN_BLOCK = 512


def _rmsnorm(x, g):
    ms = jnp.mean(x * x, axis=-1, keepdims=True)
    return x * lax.rsqrt(ms + EPS) * g


def _params(semantics):
    return pltpu.CompilerParams(dimension_semantics=semantics, vmem_limit_bytes=VMEM_LIMIT_BYTES)


def _resident(block_shape, index_map):
    return pl.BlockSpec(block_shape, index_map, pipeline_mode=pl.Buffered(1))


def _ffn_kernel(*refs, apply_final_norm, cast_next):
    h_ref, g_ref, wg_ref, wu_ref, wd_ref, fg_ref = refs[:6]
    if cast_next:
        ng_ref, nu_ref, nd_ref, o_ref, ng_out, nu_out, nd_out, u_ref = refs[6:]
    else:
        o_ref, u_ref = refs[6:]
    j = pl.program_id(1)

    @pl.when(j == 0)
    def _():
        x = h_ref[...]
        u_ref[...] = _rmsnorm(x, g_ref[...]).astype(BF16)
        o_ref[...] = x

    u = u_ref[...]
    gate = jnp.dot(u, wg_ref[...], preferred_element_type=F32)
    up = jnp.dot(u, wu_ref[...], preferred_element_type=F32)
    act = (jax.nn.silu(gate) * up * MACARON_WEIGHT).astype(BF16)
    o_ref[...] += jnp.dot(act, wd_ref[...], preferred_element_type=F32)

    if cast_next:
        ng_out[...] = ng_ref[...].astype(BF16)
        nu_out[...] = nu_ref[...].astype(BF16)
        nd_out[...] = nd_ref[...].astype(BF16)

    if apply_final_norm:

        @pl.when(j == pl.num_programs(1) - 1)
        def _():
            o_ref[...] = _rmsnorm(o_ref[...], fg_ref[...])


def _ffn(h, norms, norm_row, wg, wu, wd, final_g, apply_final_norm, nxt, name):
    m, d = h.shape
    f = wg.shape[-1]
    tm, tf = FFN_TOKEN_BLOCK, FFN_HIDDEN_BLOCK
    n_i = m // tm
    in_specs = [
        pl.BlockSpec((tm, d), lambda i, j: (i, 0)),
        pl.BlockSpec((None, 1, d), lambda i, j: (norm_row, 0, 0)),
        pl.BlockSpec((d, tf), lambda i, j: (0, j)),
        pl.BlockSpec((d, tf), lambda i, j: (0, j)),
        pl.BlockSpec((tf, d), lambda i, j: (j, 0)),
        pl.BlockSpec((1, d), lambda i, j: (0, 0)),
    ]
    out_specs = [pl.BlockSpec((tm, d), lambda i, j: (i, 0))]
    out_shape = [jax.ShapeDtypeStruct((m, d), F32)]
    args = [h, norms, wg, wu, wd, final_g]
    if nxt is not None:
        n_gate, n_up, n_down, n_layer, n_half = nxt
        ds = d // n_i
        in_specs += [
            pl.BlockSpec((None, None, ds, tf), lambda i, j: (n_layer, n_half, i, j)),
            pl.BlockSpec((None, None, ds, tf), lambda i, j: (n_layer, n_half, i, j)),
            pl.BlockSpec((None, None, tf, ds), lambda i, j: (n_layer, n_half, j, i)),
        ]
        out_specs += [
            pl.BlockSpec((ds, tf), lambda i, j: (i, j)),
            pl.BlockSpec((ds, tf), lambda i, j: (i, j)),
            pl.BlockSpec((tf, ds), lambda i, j: (j, i)),
        ]
        out_shape += [jax.ShapeDtypeStruct((d, f), BF16), jax.ShapeDtypeStruct((d, f), BF16),
                      jax.ShapeDtypeStruct((f, d), BF16)]
        args += [n_gate, n_up, n_down]
    outs = pl.pallas_call(
        functools.partial(_ffn_kernel, apply_final_norm=apply_final_norm, cast_next=nxt is not None),
        grid=(n_i, f // tf),
        in_specs=in_specs,
        out_specs=out_specs,
        out_shape=out_shape,
        scratch_shapes=[pltpu.VMEM((tm, d), BF16)],
        compiler_params=_params(("parallel", "arbitrary")),
        name=name,
    )(*args)
    return outs if nxt is not None else outs[0]


def _pool_kernel(h_ref, g_ref, w_ref, sc_ref, o_ref, buf_ref, *, blocks_per_seq, halo):
    tm = h_ref.shape[0]
    groups = len(POOL_WINDOWS)
    gd = h_ref.shape[1] // groups
    blk = pl.program_id(0) % blocks_per_seq

    @pl.when(blk == 0)
    def _():
        buf_ref[0:halo, :] = jnp.zeros((halo, buf_ref.shape[1]), F32)

    x = h_ref[...]
    buf_ref[halo:, :] = _rmsnorm(x, g_ref[...])
    pos = (blk * tm + 1 + lax.broadcasted_iota(jnp.int32, (tm, 1), 0)).astype(F32)

    for g, w in enumerate(POOL_WINDOWS):
        cols = slice(g * gd, (g + 1) * gd)
        cur = buf_ref[halo:halo + tm, cols]
        acc = cur
        for k in range(1, w):
            acc = acc + buf_ref[halo - k:halo - k + tm, cols]
        count = jnp.minimum(pos, float(w))
        pooled = (acc / count - cur).astype(BF16)
        y = jnp.dot(pooled, w_ref[g], preferred_element_type=F32)
        o_ref[:, cols] = x[:, cols] + y * sc_ref[:, cols]

    buf_ref[0:halo, :] = buf_ref[tm:tm + halo, :]


def _pool(h, norms, pool_w, pool_scale, layer, j, seq_len):
    m, d = h.shape
    tm = POOL_TOKEN_BLOCK
    halo = max(POOL_WINDOWS)
    groups, gd = pool_w.shape[1], pool_w.shape[2]
    return pl.pallas_call(
        functools.partial(_pool_kernel, blocks_per_seq=seq_len // tm, halo=halo),
        grid=(m // tm,),
        in_specs=[
            pl.BlockSpec((tm, d), lambda i: (i, 0)),
            pl.BlockSpec((None, 1, d), lambda i: (layer, 0, 0)),
            _resident((None, groups, gd, gd), lambda i: (j, 0, 0, 0)),
            pl.BlockSpec((None, 1, d), lambda i: (j, 0, 0)),
        ],
        out_specs=pl.BlockSpec((tm, d), lambda i: (i, 0)),
        out_shape=jax.ShapeDtypeStruct((m, d), F32),
        scratch_shapes=[pltpu.VMEM((halo + tm, d), F32)],
        compiler_params=_params(("arbitrary",)),
        name=f"pool_l{layer}",
    )(h, norms, pool_w, pool_scale)


def _softplus(z):
    return jnp.maximum(z, 0.0) + jnp.log1p(jnp.exp(-jnp.abs(z)))


def _lru_kernel(h_ref, g_ref, win_ref, cw_ref, cb_ref, wax_ref, ba_ref, bx_ref, lam_ref, wout_ref,
                o_ref, proj_ref, a_ref, b_ref, y_ref, carry_ref, *, blocks_per_seq, conv_width):
    tm = h_ref.shape[0]
    r = a_ref.shape[1]
    hd_dim = r // LRU_HEADS
    tail = SUBLANES_F32
    blk = pl.program_id(0) % blocks_per_seq

    @pl.when(blk == 0)
    def _():
        proj_ref[0:tail, r:] = jnp.zeros((tail, r), F32)
        carry_ref[...] = jnp.zeros(carry_ref.shape, F32)

    x = h_ref[...]
    u = _rmsnorm(x, g_ref[...]).astype(BF16)
    proj_ref[tail:, :] = jnp.dot(u, win_ref[...], preferred_element_type=F32)

    decay = _softplus(-lam_ref[...])
    for hd in range(LRU_HEADS):
        cols = slice(hd * hd_dim, (hd + 1) * hd_dim)
        xcols = slice(r + hd * hd_dim, r + (hd + 1) * hd_dim)
        xc = cb_ref[:, cols]
        for k in range(conv_width):
            back = conv_width - 1 - k
            xc = xc + proj_ref[tail - back:tail - back + tm, xcols] * cw_ref[k:k + 1, cols]
        ra = jnp.dot(xc.astype(BF16), wax_ref[hd], preferred_element_type=F32)
        rg = jax.nn.sigmoid(ra[:, :hd_dim] + ba_ref[:, cols])
        ig = jax.nn.sigmoid(ra[:, hd_dim:] + bx_ref[:, cols])
        log_a = (-LRU_C) * rg * decay[:, cols]
        a = jnp.exp(log_a)
        a_ref[:, cols] = a
        b_ref[:, cols] = jnp.sqrt(1.0 - a * a) * ig * xc

    proj_ref[0:tail, r:] = proj_ref[tm:tm + tail, r:]

    rows = lax.broadcasted_iota(jnp.int32, (SUBLANES_F32, r), 0)

    def scan_group(row0, carry):
        a = a_ref[pl.ds(row0, SUBLANES_F32), :]
        b = b_ref[pl.ds(row0, SUBLANES_F32), :]
        for dist in (1, 2, 4):
            keep = rows >= dist
            a_prev = jnp.where(keep, pltpu.roll(a, dist, 0), 1.0)
            b_prev = jnp.where(keep, pltpu.roll(b, dist, 0), 0.0)
            b = a * b_prev + b
            a = a * a_prev
        hh = b + a * carry
        gate = proj_ref[pl.ds(tail + row0, SUBLANES_F32), 0:r]
        y = hh * jax.nn.gelu(gate)
        return y, jnp.broadcast_to(hh[SUBLANES_F32 - 1:, :], (SUBLANES_F32, r))

    def scan_pair(p, carry):
        row0 = pl.multiple_of(p * SUBLANES_BF16, SUBLANES_BF16)
        y0, carry = scan_group(row0, carry)
        y1, carry = scan_group(row0 + SUBLANES_F32, carry)
        y_ref[pl.ds(row0, SUBLANES_BF16), :] = jnp.concatenate([y0, y1], axis=0).astype(BF16)
        return carry

    carry_ref[...] = lax.fori_loop(0, tm // SUBLANES_BF16, scan_pair, carry_ref[...])
    o_ref[...] = x + jnp.dot(y_ref[...], wout_ref[...], preferred_element_type=F32)


def _lru(h, norms, w_in, conv_w, conv_b, w_ax, b_a, b_x, lam, w_out, layer, j, seq_len):
    m, d = h.shape
    tm = LRU_TOKEN_BLOCK
    r = w_out.shape[1]
    conv_width = conv_w.shape[1]
    heads, hd_dim = w_ax.shape[1], w_ax.shape[2]
    vec = lambda idx: pl.BlockSpec((None, 1, r), lambda i: (idx, 0, 0))
    return pl.pallas_call(
        functools.partial(_lru_kernel, blocks_per_seq=seq_len // tm, conv_width=conv_width),
        grid=(m // tm,),
        in_specs=[
            pl.BlockSpec((tm, d), lambda i: (i, 0)),
            pl.BlockSpec((None, 1, d), lambda i: (layer, 0, 0)),
            _resident((None, d, 2 * r), lambda i: (j, 0, 0)),
            pl.BlockSpec((None, conv_width, r), lambda i: (j, 0, 0)),
            vec(j),
            _resident((None, heads, hd_dim, 2 * hd_dim), lambda i: (j, 0, 0, 0)),
            vec(j),
            vec(j),
            vec(j),
            _resident((None, r, d), lambda i: (j, 0, 0)),
        ],
        out_specs=pl.BlockSpec((tm, d), lambda i: (i, 0)),
        out_shape=jax.ShapeDtypeStruct((m, d), F32),
        scratch_shapes=[
            pltpu.VMEM((SUBLANES_F32 + tm, 2 * r), F32),
            pltpu.VMEM((tm, r), F32),
            pltpu.VMEM((tm, r), F32),
            pltpu.VMEM((tm, r), BF16),
            pltpu.VMEM((SUBLANES_F32, r), F32),
        ],
        compiler_params=_params(("arbitrary",)),
        name=f"lru_l{layer}",
    )(h, norms, w_in, conv_w, conv_b, w_ax, b_a, b_x, lam, w_out)


def _kv_kernel(m_ref, g_ref, wk_ref, wv_ref, k_ref, v_ref):
    mn = _rmsnorm(m_ref[...], g_ref[...]).astype(BF16)
    k_ref[...] = jnp.dot(mn, wk_ref[...].astype(BF16), preferred_element_type=F32).astype(BF16)
    v_ref[...] = jnp.dot(mn, wv_ref[...].astype(BF16), preferred_element_type=F32).astype(BF16)


def _kv(mem2d, mem_norm, w_k, w_v):
    rows, d = mem2d.shape
    depth = w_k.shape[0]
    tn = KV_COLUMN_BLOCK
    out = jax.ShapeDtypeStruct((depth, rows, d), BF16)
    return pl.pallas_call(
        _kv_kernel,
        grid=(depth, d // tn),
        in_specs=[
            _resident((rows, d), lambda l, n: (0, 0)),
            pl.BlockSpec((1, d), lambda l, n: (0, 0)),
            pl.BlockSpec((None, d, tn), lambda l, n: (l, 0, n)),
            pl.BlockSpec((None, d, tn), lambda l, n: (l, 0, n)),
        ],
        out_specs=[pl.BlockSpec((None, rows, tn), lambda l, n: (l, 0, n))] * 2,
        out_shape=[out, out],
        compiler_params=_params(("arbitrary", "arbitrary")),
        name="kv_proj",
    )(mem2d, mem_norm, w_k, w_v)


def _xattn_kernel(h_ref, g_ref, wq_ref, k_ref, v_ref, wo_ref, o_ref, q_ref, a_ref):
    d = h_ref.shape[1]
    hd_dim = d // XATTN_HEADS
    scale = hd_dim ** -0.5
    x = h_ref[...]
    u = _rmsnorm(x, g_ref[...]).astype(BF16)
    q_ref[...] = jnp.dot(u, wq_ref[...], preferred_element_type=F32).astype(BF16)
    for hd in range(XATTN_HEADS):
        cols = slice(hd * hd_dim, (hd + 1) * hd_dim)
        s = lax.dot_general(q_ref[:, cols], k_ref[:, cols], (((1,), (1,)), ((), ())),
                            preferred_element_type=F32) * scale
        e = jnp.exp(s - jnp.max(s, axis=-1, keepdims=True))
        p = (e / jnp.sum(e, axis=-1, keepdims=True)).astype(BF16)
        a_ref[:, cols] = jnp.dot(p, v_ref[:, cols], preferred_element_type=F32).astype(BF16)
    o_ref[...] = x + jnp.dot(a_ref[...], wo_ref[...], preferred_element_type=F32)


def _xattn(h, norms, w_q, k_all, v_all, w_o, layer, seq_len, mem_len):
    m, d = h.shape
    tm = XATTN_TOKEN_BLOCK
    blocks_per_seq = seq_len // tm
    return pl.pallas_call(
        _xattn_kernel,
        grid=(m // tm,),
        in_specs=[
            pl.BlockSpec((tm, d), lambda i: (i, 0)),
            pl.BlockSpec((None, 1, d), lambda i: (layer, 0, 0)),
            _resident((None, d, d), lambda i: (layer, 0, 0)),
            pl.BlockSpec((None, mem_len, d), lambda i: (layer, i // blocks_per_seq, 0)),
            pl.BlockSpec((None, mem_len, d), lambda i: (layer, i // blocks_per_seq, 0)),
            _resident((None, d, d), lambda i: (layer, 0, 0)),
        ],
        out_specs=pl.BlockSpec((tm, d), lambda i: (i, 0)),
        out_shape=jax.ShapeDtypeStruct((m, d), F32),
        scratch_shapes=[pltpu.VMEM((tm, d), BF16), pltpu.VMEM((tm, d), BF16)],
        compiler_params=_params(("arbitrary",)),
        name=f"xattn_l{layer}",
    )(h, norms, w_q, k_all, v_all, w_o)


def kernel(x, mem, ffn_norm, w_ffn_gate, w_ffn_up, w_ffn_down, mix_norm, pool_w, pool_scale, lru_w_in, lru_conv_w, lru_conv_b, lru_w_a, lru_b_a, lru_w_x, lru_b_x, lru_lambda, lru_w_out, xattn_norm, mem_norm, w_q, w_k, w_v, w_o, final_norm):
    batch, seq_len, d = x.shape
    mem_len = mem.shape[1]
    depth = ffn_norm.shape[0]
    n_mixers = 2

    row = lambda a: a.reshape(-1, 1, a.shape[-1])
    ffn_norm_r, mix_norm_r, xattn_norm_r = row(ffn_norm), row(mix_norm), row(xattn_norm)
    pool_scale_r = row(pool_scale)
    conv_b_r, b_a_r, b_x_r, lam_r = row(lru_conv_b), row(lru_b_a), row(lru_b_x), row(lru_lambda)
    final_g = final_norm.reshape(1, d)

    ffn_w = (w_ffn_gate[0, 0].astype(BF16), w_ffn_up[0, 0].astype(BF16), w_ffn_down[0, 0].astype(BF16))
    pool_w_b = pool_w.astype(BF16)
    w_in_b, w_out_b = lru_w_in.astype(BF16), lru_w_out.astype(BF16)
    w_ax = jnp.concatenate([lru_w_a, lru_w_x], axis=-1).astype(BF16)
    w_q_b, w_o_b = w_q.astype(BF16), w_o.astype(BF16)

    k_all, v_all = _kv(mem.reshape(batch * mem_len, d), mem_norm.reshape(1, d),
                       w_k, w_v)

    def ffn(h, ffn_w, layer, half):
        last = layer == depth - 1 and half == 1
        nxt = None if last else (w_ffn_gate, w_ffn_up, w_ffn_down) + ((layer, 1) if half == 0 else (layer + 1, 0))
        outs = _ffn(h, ffn_norm_r, layer * 2 + half, *ffn_w, final_g, last, nxt, f"ffn_l{layer}_{half}")
        return (outs, None) if last else (outs[0], tuple(outs[1:]))

    h = x.reshape(batch * seq_len, d)
    for i in range(depth):
        j = i // n_mixers
        h, ffn_w = ffn(h, ffn_w, i, 0)
        if i % n_mixers == 0:
            h = _pool(h, mix_norm_r, pool_w_b, pool_scale_r, i, j, seq_len)
        else:
            h = _lru(h, mix_norm_r, w_in_b, lru_conv_w, conv_b_r, w_ax, b_a_r, b_x_r, lam_r, w_out_b,
                     i, j, seq_len)
        h = _xattn(h, xattn_norm_r, w_q_b, k_all, v_all, w_o_b, i, seq_len, mem_len)
        h, ffn_w = ffn(h, ffn_w, i, 1)
    return h.reshape(batch, seq_len, d)
```

```python
import functools

import jax
import jax.numpy as jnp
from jax import lax
from jax.experimental import pallas as pl
from jax.experimental.pallas import tpu as pltpu

F32 = jnp.float32
BF16 = jnp.bfloat16

EPS = 1e-6
MACARON_WEIGHT = 0.5
POOL_WINDOWS = (2, 4, 8, 16)
LRU_HEADS = 16
LRU_C = 8.0
XATTN_HEADS = 4

SUBLANES_F32 = 8
SUBLANES_BF16 = 16
VMEM_LIMIT_BYTES = 58 * 1024 * 1024

FFN_TOKEN_BLOCK = 1024
FFN_HIDDEN_BLOCK = 512
POOL_TOKEN_BLOCK = 512
LRU_TOKEN_BLOCK = 256
LRU_PROJ_CHUNKS = 8
XATTN_TOKEN_BLOCK = 512
KV_COLUMN_BLOCK = 512


def _rmsnorm(x, g):
    ms = jnp.mean(x * x, axis=-1, keepdims=True)
    return x * lax.rsqrt(ms + EPS) * g


def _params(semantics):
    return pltpu.CompilerParams(dimension_semantics=semantics, vmem_limit_bytes=VMEM_LIMIT_BYTES)


def _resident(block_shape, index_map):
    return pl.BlockSpec(block_shape, index_map, pipeline_mode=pl.Buffered(1))


def _ffn_kernel(*refs, apply_final_norm, cast_next):
    h_ref, g_ref, wg_ref, wu_ref, wd_ref, fg_ref = refs[:6]
    if cast_next:
        ng_ref, nu_ref, nd_ref, o_ref, ng_out, nu_out, nd_out, u_ref = refs[6:]
    else:
        o_ref, u_ref = refs[6:]
    j = pl.program_id(1)

    @pl.when(j == 0)
    def _():
        x = h_ref[...]
        u_ref[...] = _rmsnorm(x, g_ref[...]).astype(BF16)
        o_ref[...] = x

    u = u_ref[...]
    gate = jnp.dot(u, wg_ref[...], preferred_element_type=F32)
    up = jnp.dot(u, wu_ref[...], preferred_element_type=F32)
    act = (jax.nn.silu(gate) * up * MACARON_WEIGHT).astype(BF16)
    o_ref[...] += jnp.dot(act, wd_ref[...], preferred_element_type=F32)

    if cast_next:
        ng_out[...] = ng_ref[...].astype(BF16)
        nu_out[...] = nu_ref[...].astype(BF16)
        nd_out[...] = nd_ref[...].astype(BF16)

    if apply_final_norm:

        @pl.when(j == pl.num_programs(1) - 1)
        def _():
            o_ref[...] = _rmsnorm(o_ref[...], fg_ref[...])


def _ffn(h, norms, norm_row, wg, wu, wd, final_g, apply_final_norm, nxt, name):
    m, d = h.shape
    f = wg.shape[-1]
    tm, tf = FFN_TOKEN_BLOCK, FFN_HIDDEN_BLOCK
    n_i = m // tm
    in_specs = [
        pl.BlockSpec((tm, d), lambda i, j: (i, 0)),
        pl.BlockSpec((None, 1, d), lambda i, j: (norm_row, 0, 0)),
        pl.BlockSpec((d, tf), lambda i, j: (0, j)),
        pl.BlockSpec((d, tf), lambda i, j: (0, j)),
        pl.BlockSpec((tf, d), lambda i, j: (j, 0)),
        pl.BlockSpec((1, d), lambda i, j: (0, 0)),
    ]
    out_specs = [pl.BlockSpec((tm, d), lambda i, j: (i, 0))]
    out_shape = [jax.ShapeDtypeStruct((m, d), F32)]
    args = [h, norms, wg, wu, wd, final_g]
    if nxt is not None:
        n_gate, n_up, n_down, n_layer, n_half = nxt
        ds = d // n_i
        in_specs += [
            pl.BlockSpec((None, None, ds, tf), lambda i, j: (n_layer, n_half, i, j)),
            pl.BlockSpec((None, None, ds, tf), lambda i, j: (n_layer, n_half, i, j)),
            pl.BlockSpec((None, None, tf, ds), lambda i, j: (n_layer, n_half, j, i)),
        ]
        out_specs += [
            pl.BlockSpec((ds, tf), lambda i, j: (i, j)),
            pl.BlockSpec((ds, tf), lambda i, j: (i, j)),
            pl.BlockSpec((tf, ds), lambda i, j: (j, i)),
        ]
        out_shape += [jax.ShapeDtypeStruct((d, f), BF16), jax.ShapeDtypeStruct((d, f), BF16),
                      jax.ShapeDtypeStruct((f, d), BF16)]
        args += [n_gate, n_up, n_down]
    outs = pl.pallas_call(
        functools.partial(_ffn_kernel, apply_final_norm=apply_final_norm, cast_next=nxt is not None),
        grid=(n_i, f // tf),
        in_specs=in_specs,
        out_specs=out_specs,
        out_shape=out_shape,
        scratch_shapes=[pltpu.VMEM((tm, d), BF16)],
        compiler_params=_params(("parallel", "arbitrary")),
        name=name,
    )(*args)
    return outs if nxt is not None else outs[0]


def _pool_kernel(h_ref, g_ref, w_ref, sc_ref, o_ref, buf_ref, *, blocks_per_seq, halo):
    tm = h_ref.shape[0]
    groups = len(POOL_WINDOWS)
    gd = h_ref.shape[1] // groups
    blk = pl.program_id(0) % blocks_per_seq

    @pl.when(blk == 0)
    def _():
        buf_ref[0:halo, :] = jnp.zeros((halo, buf_ref.shape[1]), F32)

    x = h_ref[...]
    buf_ref[halo:, :] = _rmsnorm(x, g_ref[...])
    pos = (blk * tm + 1 + lax.broadcasted_iota(jnp.int32, (tm, 1), 0)).astype(F32)

    for g, w in enumerate(POOL_WINDOWS):
        cols = slice(g * gd, (g + 1) * gd)
        cur = buf_ref[halo:halo + tm, cols]
        acc = cur
        for k in range(1, w):
            acc = acc + buf_ref[halo - k:halo - k + tm, cols]
        count = jnp.minimum(pos, float(w))
        pooled = (acc / count - cur).astype(BF16)
        y = jnp.dot(pooled, w_ref[g], preferred_element_type=F32)
        o_ref[:, cols] = x[:, cols] + y * sc_ref[:, cols]

    buf_ref[0:halo, :] = buf_ref[tm:tm + halo, :]


def _pool(h, norms, pool_w, pool_scale, layer, j, seq_len):
    m, d = h.shape
    tm = POOL_TOKEN_BLOCK
    halo = max(POOL_WINDOWS)
    groups, gd = pool_w.shape[1], pool_w.shape[2]
    return pl.pallas_call(
        functools.partial(_pool_kernel, blocks_per_seq=seq_len // tm, halo=halo),
        grid=(m // tm,),
        in_specs=[
            pl.BlockSpec((tm, d), lambda i: (i, 0)),
            pl.BlockSpec((None, 1, d), lambda i: (layer, 0, 0)),
            _resident((None, groups, gd, gd), lambda i: (j, 0, 0, 0)),
            pl.BlockSpec((None, 1, d), lambda i: (j, 0, 0)),
        ],
        out_specs=pl.BlockSpec((tm, d), lambda i: (i, 0)),
        out_shape=jax.ShapeDtypeStruct((m, d), F32),
        scratch_shapes=[pltpu.VMEM((halo + tm, d), F32)],
        compiler_params=_params(("arbitrary",)),
        name=f"pool_l{layer}",
    )(h, norms, pool_w, pool_scale)


def _softplus(z):
    return jnp.maximum(z, 0.0) + jnp.log1p(jnp.exp(-jnp.abs(z)))


def _sigmoid(z):
    return 0.5 * jnp.tanh(0.5 * z) + 0.5


def _lru_kernel(h_ref, hlag_ref, g_ref, win_ref, cw_ref, cb_ref, wax_ref, ba_ref, bx_ref, lam_ref, wout_ref,
                o_ref, proj0_ref, proj1_ref, a_ref, b_ref, y_ref, carry_ref, *, blocks_per_seq, conv_width):
    tm = h_ref.shape[0]
    r = a_ref.shape[1]
    hd_dim = r // LRU_HEADS
    tail = SUBLANES_F32
    s = pl.program_id(0)
    drain_starts_seq = (s + blocks_per_seq - 1) % blocks_per_seq == 0
    fill_starts_seq = s % blocks_per_seq == 0

    @pl.when(s == 0)
    def _():
        proj0_ref[...] = jnp.zeros(proj0_ref.shape, F32)
        proj1_ref[...] = jnp.zeros(proj1_ref.shape, F32)
        carry_ref[...] = jnp.zeros(carry_ref.shape, F32)

    def step(fill_ref, drain_ref):
        u = _rmsnorm(h_ref[...], g_ref[...]).astype(BF16)
        decay = _softplus(-lam_ref[...])
        rows = lax.broadcasted_iota(jnp.int32, (SUBLANES_F32, r), 0)
        chunk = 2 * r // LRU_PROJ_CHUNKS

        def project(c):
            cols = slice(c * chunk, (c + 1) * chunk)
            fill_ref[tail:, cols] = jnp.dot(u, win_ref[:, cols], preferred_element_type=F32)

        def gates(hd):
            cols = slice(hd * hd_dim, (hd + 1) * hd_dim)
            xcols = slice(r + hd * hd_dim, r + (hd + 1) * hd_dim)
            xc = cb_ref[:, cols]
            for k in range(conv_width):
                back = conv_width - 1 - k
                xc = xc + drain_ref[tail - back:tail - back + tm, xcols] * cw_ref[k:k + 1, cols]
            ra = jnp.dot(xc.astype(BF16), wax_ref[hd], preferred_element_type=F32)
            rg = _sigmoid(ra[:, :hd_dim] + ba_ref[:, cols])
            ig = _sigmoid(ra[:, hd_dim:] + bx_ref[:, cols])
            a = jnp.exp((-LRU_C) * rg * decay[:, cols])
            v = 1.0 - a * a
            a_ref[:, cols] = a
            b_ref[:, cols] = jnp.where(v > 0.0, v * lax.rsqrt(v), 0.0) * ig * xc

        def scan_group(row0, carry):
            a = a_ref[row0:row0 + SUBLANES_F32, :]
            b = b_ref[row0:row0 + SUBLANES_F32, :]
            for dist in (1, 2, 4):
                keep = rows >= dist
                a_prev = jnp.where(keep, pltpu.roll(a, dist, 0), 1.0)
                b_prev = jnp.where(keep, pltpu.roll(b, dist, 0), 0.0)
                b = a * b_prev + b
                a = a * a_prev
            hh = b + a * carry
            gate = drain_ref[tail + row0:tail + row0 + SUBLANES_F32, 0:r]
            y = hh * jax.nn.gelu(gate)
            return y, jnp.broadcast_to(hh[SUBLANES_F32 - 1:, :], (SUBLANES_F32, r))

        state = {"carry": jnp.where(drain_starts_seq, 0.0, carry_ref[...])}

        def scan_pair(row0):
            y0, carry = scan_group(row0, state["carry"])
            y1, carry = scan_group(row0 + SUBLANES_F32, carry)
            y_ref[row0:row0 + SUBLANES_BF16, :] = jnp.concatenate([y0, y1], axis=0).astype(BF16)
            state["carry"] = carry

        vector_tasks = [functools.partial(gates, hd) for hd in range(LRU_HEADS)]
        vector_tasks += [functools.partial(scan_pair, row0) for row0 in range(0, tm, SUBLANES_BF16)]
        per_chunk = -(-len(vector_tasks) // LRU_PROJ_CHUNKS)
        for c in range(LRU_PROJ_CHUNKS):
            project(c)
            for task in vector_tasks[c * per_chunk:(c + 1) * per_chunk]:
                task()

        fill_ref[0:tail, r:] = jnp.where(fill_starts_seq, 0.0, drain_ref[tm:tm + tail, r:])
        carry_ref[...] = state["carry"]
        o_ref[...] = hlag_ref[...] + jnp.dot(y_ref[...], wout_ref[...], preferred_element_type=F32)

    @pl.when(s % 2 == 0)
    def _():
        step(proj0_ref, proj1_ref)

    @pl.when(s % 2 == 1)
    def _():
        step(proj1_ref, proj0_ref)


def _lru(h, norms, w_in, conv_w, conv_b, w_ax, b_a, b_x, lam, w_out, layer, j, seq_len):
    m, d = h.shape
    tm = LRU_TOKEN_BLOCK
    n_blocks = m // tm
    r = w_out.shape[1]
    conv_width = conv_w.shape[1]
    heads, hd_dim = w_ax.shape[1], w_ax.shape[2]
    vec = lambda idx: pl.BlockSpec((None, 1, r), lambda s: (idx, 0, 0))
    lagged = lambda s: (jnp.maximum(s - 1, 0), 0)
    proj = pltpu.VMEM((SUBLANES_F32 + tm, 2 * r), F32)
    return pl.pallas_call(
        functools.partial(_lru_kernel, blocks_per_seq=seq_len // tm, conv_width=conv_width),
        grid=(n_blocks + 1,),
        in_specs=[
            pl.BlockSpec((tm, d), lambda s: (jnp.minimum(s, n_blocks - 1), 0)),
            pl.BlockSpec((tm, d), lagged),
            pl.BlockSpec((None, 1, d), lambda s: (layer, 0, 0)),
            _resident((None, d, 2 * r), lambda s: (j, 0, 0)),
            pl.BlockSpec((None, conv_width, r), lambda s: (j, 0, 0)),
            vec(j),
            _resident((None, heads, hd_dim, 2 * hd_dim), lambda s: (j, 0, 0, 0)),
            vec(j),
            vec(j),
            vec(j),
            _resident((None, r, d), lambda s: (j, 0, 0)),
        ],
        out_specs=pl.BlockSpec((tm, d), lagged),
        out_shape=jax.ShapeDtypeStruct((m, d), F32),
        scratch_shapes=[
            proj,
            proj,
            pltpu.VMEM((tm, r), F32),
            pltpu.VMEM((tm, r), F32),
            pltpu.VMEM((tm, r), BF16),
            pltpu.VMEM((SUBLANES_F32, r), F32),
        ],
        compiler_params=_params(("arbitrary",)),
        name=f"lru_l{layer}",
    )(h, h, norms, w_in, conv_w, conv_b, w_ax, b_a, b_x, lam, w_out)


def _kv_kernel(m_ref, g_ref, wk_ref, wv_ref, k_ref, v_ref):
    mn = _rmsnorm(m_ref[...], g_ref[...]).astype(BF16)
    k_ref[...] = jnp.dot(mn, wk_ref[...].astype(BF16), preferred_element_type=F32).astype(BF16)
    v_ref[...] = jnp.dot(mn, wv_ref[...].astype(BF16), preferred_element_type=F32).astype(BF16)


def _kv(mem2d, mem_norm, w_k, w_v):
    rows, d = mem2d.shape
    depth = w_k.shape[0]
    tn = KV_COLUMN_BLOCK
    out = jax.ShapeDtypeStruct((depth, rows, d), BF16)
    return pl.pallas_call(
        _kv_kernel,
        grid=(depth, d // tn),
        in_specs=[
            _resident((rows, d), lambda l, n: (0, 0)),
            pl.BlockSpec((1, d), lambda l, n: (0, 0)),
            pl.BlockSpec((None, d, tn), lambda l, n: (l, 0, n)),
            pl.BlockSpec((None, d, tn), lambda l, n: (l, 0, n)),
        ],
        out_specs=[pl.BlockSpec((None, rows, tn), lambda l, n: (l, 0, n))] * 2,
        out_shape=[out, out],
        compiler_params=_params(("arbitrary", "arbitrary")),
        name="kv_proj",
    )(mem2d, mem_norm, w_k, w_v)


def _xattn_kernel(h_ref, g_ref, wq_ref, k_ref, v_ref, wo_ref, o_ref, q_ref, a_ref):
    d = h_ref.shape[1]
    hd_dim = d // XATTN_HEADS
    scale = hd_dim ** -0.5
    x = h_ref[...]
    u = _rmsnorm(x, g_ref[...]).astype(BF16)
    q_ref[...] = jnp.dot(u, wq_ref[...], preferred_element_type=F32).astype(BF16)
    for hd in range(XATTN_HEADS):
        cols = slice(hd * hd_dim, (hd + 1) * hd_dim)
        s = lax.dot_general(q_ref[:, cols], k_ref[:, cols], (((1,), (1,)), ((), ())),
                            preferred_element_type=F32) * scale
        e = jnp.exp(s - jnp.max(s, axis=-1, keepdims=True))
        p = (e / jnp.sum(e, axis=-1, keepdims=True)).astype(BF16)
        a_ref[:, cols] = jnp.dot(p, v_ref[:, cols], preferred_element_type=F32).astype(BF16)
    o_ref[...] = x + jnp.dot(a_ref[...], wo_ref[...], preferred_element_type=F32)


def _xattn(h, norms, w_q, k_all, v_all, w_o, layer, seq_len, mem_len):
    m, d = h.shape
    tm = XATTN_TOKEN_BLOCK
    blocks_per_seq = seq_len // tm
    return pl.pallas_call(
        _xattn_kernel,
        grid=(m // tm,),
        in_specs=[
            pl.BlockSpec((tm, d), lambda i: (i, 0)),
            pl.BlockSpec((None, 1, d), lambda i: (layer, 0, 0)),
            _resident((None, d, d), lambda i: (layer, 0, 0)),
            pl.BlockSpec((None, mem_len, d), lambda i: (layer, i // blocks_per_seq, 0)),
            pl.BlockSpec((None, mem_len, d), lambda i: (layer, i // blocks_per_seq, 0)),
            _resident((None, d, d), lambda i: (layer, 0, 0)),
        ],
        out_specs=pl.BlockSpec((tm, d), lambda i: (i, 0)),
        out_shape=jax.ShapeDtypeStruct((m, d), F32),
        scratch_shapes=[pltpu.VMEM((tm, d), BF16), pltpu.VMEM((tm, d), BF16)],
        compiler_params=_params(("arbitrary",)),
        name=f"xattn_l{layer}",
    )(h, norms, w_q, k_all, v_all, w_o)


def kernel(x, mem, ffn_norm, w_ffn_gate, w_ffn_up, w_ffn_down, mix_norm, pool_w, pool_scale, lru_w_in, lru_conv_w, lru_conv_b, lru_w_a, lru_b_a, lru_w_x, lru_b_x, lru_lambda, lru_w_out, xattn_norm, mem_norm, w_q, w_k, w_v, w_o, final_norm):
    batch, seq_len, d = x.shape
    mem_len = mem.shape[1]
    depth = ffn_norm.shape[0]
    n_mixers = 2

    row = lambda a: a.reshape(-1, 1, a.shape[-1])
    ffn_norm_r, mix_norm_r, xattn_norm_r = row(ffn_norm), row(mix_norm), row(xattn_norm)
    pool_scale_r = row(pool_scale)
    conv_b_r, b_a_r, b_x_r, lam_r = row(lru_conv_b), row(lru_b_a), row(lru_b_x), row(lru_lambda)
    final_g = final_norm.reshape(1, d)

    ffn_w = (w_ffn_gate[0, 0].astype(BF16), w_ffn_up[0, 0].astype(BF16), w_ffn_down[0, 0].astype(BF16))
    pool_w_b = pool_w.astype(BF16)
    w_in_b, w_out_b = lru_w_in.astype(BF16), lru_w_out.astype(BF16)
    w_ax = jnp.concatenate([lru_w_a, lru_w_x], axis=-1).astype(BF16)
    w_q_b, w_o_b = w_q.astype(BF16), w_o.astype(BF16)

    k_all, v_all = _kv(mem.reshape(batch * mem_len, d), mem_norm.reshape(1, d), w_k, w_v)

    def ffn(h, ffn_w, layer, half):
        last = layer == depth - 1 and half == 1
        nxt = None if last else (w_ffn_gate, w_ffn_up, w_ffn_down) + ((layer, 1) if half == 0 else (layer + 1, 0))
        outs = _ffn(h, ffn_norm_r, layer * 2 + half, *ffn_w, final_g, last, nxt, f"ffn_l{layer}_{half}")
        return (outs, None) if last else (outs[0], tuple(outs[1:]))

    h = x.reshape(batch * seq_len, d)
    for i in range(depth):
        j = i // n_mixers
        h, ffn_w = ffn(h, ffn_w, i, 0)
        if i % n_mixers == 0:
            h = _pool(h, mix_norm_r, pool_w_b, pool_scale_r, i, j, seq_len)
        else:
            h = _lru(h, mix_norm_r, w_in_b, lru_conv_w, conv_b_r, w_ax, b_a_r, b_x_r, lam_r, w_out_b,
                     i, j, seq_len)
        h = _xattn(h, xattn_norm_r, w_q_b, k_all, v_all, w_o_b, i, seq_len, mem_len)
        h, ffn_w = ffn(h, ffn_w, i, 1)
    return h.reshape(batch, seq_len, d)
```

```python
import functools

import jax
import jax.numpy as jnp
from jax import lax
from jax.experimental import pallas as pl
from jax.experimental.pallas import tpu as pltpu

F32 = jnp.float32
BF16 = jnp.bfloat16

EPS = 1e-6
MACARON_WEIGHT = 0.5
POOL_WINDOWS = (2, 4, 8, 16)
LRU_HEADS = 16
LRU_C = 8.0
XATTN_HEADS = 4

SUBLANES_F32 = 8
SUBLANES_BF16 = 16
VMEM_LIMIT_BYTES = 58 * 1024 * 1024

FFN_TOKEN_BLOCK = 1024
FFN_HIDDEN_BLOCK = 512
POOL_TOKEN_BLOCK = 512
LRU_TOKEN_BLOCK = 256
LRU_PROJ_CHUNKS = 8
XATTN_TOKEN_BLOCK = 512
KV_COLUMN_BLOCK = 512


def _rmsnorm(x, g):
    ms = jnp.mean(x * x, axis=-1, keepdims=True)
    return x * lax.rsqrt(ms + EPS) * g


def _params(semantics):
    return pltpu.CompilerParams(dimension_semantics=semantics, vmem_limit_bytes=VMEM_LIMIT_BYTES)


def _resident(block_shape, index_map):
    return pl.BlockSpec(block_shape, index_map, pipeline_mode=pl.Buffered(1))


def _ffn_kernel(*refs, apply_final_norm, cast_next):
    h_ref, g_ref, wg_ref, wu_ref, wd_ref, fg_ref = refs[:6]
    if cast_next:
        ng_ref, nu_ref, nd_ref, o_ref, ng_out, nu_out, nd_out, u_ref = refs[6:]
    else:
        o_ref, u_ref = refs[6:]
    j = pl.program_id(1)

    @pl.when(j == 0)
    def _():
        x = h_ref[...]
        u_ref[...] = _rmsnorm(x, g_ref[...]).astype(BF16)
        o_ref[...] = x

    u = u_ref[...]
    gate = jnp.dot(u, wg_ref[...], preferred_element_type=F32)
    up = jnp.dot(u, wu_ref[...], preferred_element_type=F32)
    act = (jax.nn.silu(gate) * up * MACARON_WEIGHT).astype(BF16)
    o_ref[...] += jnp.dot(act, wd_ref[...], preferred_element_type=F32)

    if cast_next:
        ng_out[...] = ng_ref[...].astype(BF16)
        nu_out[...] = nu_ref[...].astype(BF16)
        nd_out[...] = nd_ref[...].astype(BF16)

    if apply_final_norm:

        @pl.when(j == pl.num_programs(1) - 1)
        def _():
            o_ref[...] = _rmsnorm(o_ref[...], fg_ref[...])


def _ffn(h, norms, norm_row, wg, wu, wd, final_g, apply_final_norm, nxt, name):
    m, d = h.shape
    f = wg.shape[-1]
    tm, tf = FFN_TOKEN_BLOCK, FFN_HIDDEN_BLOCK
    n_i = m // tm
    in_specs = [
        pl.BlockSpec((tm, d), lambda i, j: (i, 0)),
        pl.BlockSpec((None, 1, d), lambda i, j: (norm_row, 0, 0)),
        pl.BlockSpec((d, tf), lambda i, j: (0, j)),
        pl.BlockSpec((d, tf), lambda i, j: (0, j)),
        pl.BlockSpec((tf, d), lambda i, j: (j, 0)),
        pl.BlockSpec((1, d), lambda i, j: (0, 0)),
    ]
    out_specs = [pl.BlockSpec((tm, d), lambda i, j: (i, 0))]
    out_shape = [jax.ShapeDtypeStruct((m, d), F32)]
    args = [h, norms, wg, wu, wd, final_g]
    if nxt is not None:
        n_gate, n_up, n_down, n_layer, n_half = nxt
        ds = d // n_i
        in_specs += [
            pl.BlockSpec((None, None, ds, tf), lambda i, j: (n_layer, n_half, i, j)),
            pl.BlockSpec((None, None, ds, tf), lambda i, j: (n_layer, n_half, i, j)),
            pl.BlockSpec((None, None, tf, ds), lambda i, j: (n_layer, n_half, j, i)),
        ]
        out_specs += [
            pl.BlockSpec((ds, tf), lambda i, j: (i, j)),
            pl.BlockSpec((ds, tf), lambda i, j: (i, j)),
            pl.BlockSpec((tf, ds), lambda i, j: (j, i)),
        ]
        out_shape += [jax.ShapeDtypeStruct((d, f), BF16), jax.ShapeDtypeStruct((d, f), BF16),
                      jax.ShapeDtypeStruct((f, d), BF16)]
        args += [n_gate, n_up, n_down]
    outs = pl.pallas_call(
        functools.partial(_ffn_kernel, apply_final_norm=apply_final_norm, cast_next=nxt is not None),
        grid=(n_i, f // tf),
        in_specs=in_specs,
        out_specs=out_specs,
        out_shape=out_shape,
        scratch_shapes=[pltpu.VMEM((tm, d), BF16)],
        compiler_params=_params(("parallel", "arbitrary")),
        name=name,
    )(*args)
    return outs if nxt is not None else outs[0]


def _pool_kernel(h_ref, g_ref, w_ref, sc_ref, o_ref, buf_ref, tmp0_ref, tmp1_ref, *, blocks_per_seq, halo):
    tm = h_ref.shape[0]
    groups = len(POOL_WINDOWS)
    gd = h_ref.shape[1] // groups
    pad = 2 * halo
    rows = pad + tm
    blk = pl.program_id(0) % blocks_per_seq

    @pl.when(pl.program_id(0) == 0)
    def _():
        buf_ref[0:halo, :] = jnp.zeros((halo, buf_ref.shape[1]), F32)
        tmp0_ref[...] = jnp.zeros(tmp0_ref.shape, F32)
        tmp1_ref[...] = jnp.zeros(tmp1_ref.shape, F32)

    @pl.when(blk == 0)
    def _():
        buf_ref[halo:pad, :] = jnp.zeros((halo, buf_ref.shape[1]), F32)

    x = h_ref[...]
    buf_ref[pad:, :] = _rmsnorm(x, g_ref[...])
    pos = (blk * tm + 1 + lax.broadcasted_iota(jnp.int32, (tm, 1), 0)).astype(F32)

    for g, w in enumerate(POOL_WINDOWS):
        cols = slice(g * gd, (g + 1) * gd)
        src, src_cols = buf_ref, cols
        shift = 1
        for dst in (tmp0_ref, tmp1_ref, tmp0_ref, tmp1_ref):
            if shift >= w:
                break
            dst[halo:, :] = src[halo:rows, src_cols] + src[halo - shift:rows - shift, src_cols]
            src, src_cols = dst, slice(None)
            shift *= 2
        count = jnp.minimum(pos, float(w))
        pooled = (src[pad:, src_cols] / count - buf_ref[pad:, cols]).astype(BF16)
        y = jnp.dot(pooled, w_ref[g], preferred_element_type=F32)
        o_ref[:, cols] = x[:, cols] + y * sc_ref[:, cols]

    buf_ref[halo:pad, :] = buf_ref[tm + halo:tm + pad, :]


def _pool(h, norms, pool_w, pool_scale, layer, j, seq_len):
    m, d = h.shape
    tm = POOL_TOKEN_BLOCK
    halo = max(POOL_WINDOWS)
    groups, gd = pool_w.shape[1], pool_w.shape[2]
    return pl.pallas_call(
        functools.partial(_pool_kernel, blocks_per_seq=seq_len // tm, halo=halo),
        grid=(m // tm,),
        in_specs=[
            pl.BlockSpec((tm, d), lambda i: (i, 0)),
            pl.BlockSpec((None, 1, d), lambda i: (layer, 0, 0)),
            _resident((None, groups, gd, gd), lambda i: (j, 0, 0, 0)),
            pl.BlockSpec((None, 1, d), lambda i: (j, 0, 0)),
        ],
        out_specs=pl.BlockSpec((tm, d), lambda i: (i, 0)),
        out_shape=jax.ShapeDtypeStruct((m, d), F32),
        scratch_shapes=[
            pltpu.VMEM((2 * halo + tm, d), F32),
            pltpu.VMEM((2 * halo + tm, gd), F32),
            pltpu.VMEM((2 * halo + tm, gd), F32),
        ],
        compiler_params=_params(("arbitrary",)),
        name=f"pool_l{layer}",
    )(h, norms, pool_w, pool_scale)


def _softplus(z):
    return jnp.maximum(z, 0.0) + jnp.log1p(jnp.exp(-jnp.abs(z)))


def _sigmoid(z):
    return 0.5 * jnp.tanh(0.5 * z) + 0.5


def _lru_kernel(h_ref, hlag_ref, g_ref, win_ref, cw_ref, cb_ref, wax_ref, ba_ref, bx_ref, lam_ref, wout_ref,
                o_ref, proj0_ref, proj1_ref, a_ref, b_ref, y_ref, carry_ref, *, blocks_per_seq, conv_width):
    tm = h_ref.shape[0]
    r = a_ref.shape[1]
    hd_dim = r // LRU_HEADS
    tail = SUBLANES_F32
    s = pl.program_id(0)
    drain_starts_seq = (s + blocks_per_seq - 1) % blocks_per_seq == 0
    fill_starts_seq = s % blocks_per_seq == 0

    @pl.when(s == 0)
    def _():
        proj0_ref[...] = jnp.zeros(proj0_ref.shape, F32)
        proj1_ref[...] = jnp.zeros(proj1_ref.shape, F32)
        carry_ref[...] = jnp.zeros(carry_ref.shape, F32)

    def step(fill_ref, drain_ref):
        u = _rmsnorm(h_ref[...], g_ref[...]).astype(BF16)
        decay = _softplus(-lam_ref[...])
        rows = lax.broadcasted_iota(jnp.int32, (SUBLANES_F32, r), 0)
        chunk = 2 * r // LRU_PROJ_CHUNKS

        def project(c):
            cols = slice(c * chunk, (c + 1) * chunk)
            fill_ref[tail:, cols] = jnp.dot(u, win_ref[:, cols], preferred_element_type=F32)

        def gates(hd):
            cols = slice(hd * hd_dim, (hd + 1) * hd_dim)
            xcols = slice(r + hd * hd_dim, r + (hd + 1) * hd_dim)
            xc = cb_ref[:, cols]
            for k in range(conv_width):
                back = conv_width - 1 - k
                xc = xc + drain_ref[tail - back:tail - back + tm, xcols] * cw_ref[k:k + 1, cols]
            ra = jnp.dot(xc.astype(BF16), wax_ref[hd], preferred_element_type=F32)
            rg = _sigmoid(ra[:, :hd_dim] + ba_ref[:, cols])
            ig = _sigmoid(ra[:, hd_dim:] + bx_ref[:, cols])
            a = jnp.exp((-LRU_C) * rg * decay[:, cols])
            v = 1.0 - a * a
            a_ref[:, cols] = a
            b_ref[:, cols] = jnp.where(v > 0.0, v * lax.rsqrt(v), 0.0) * ig * xc

        def scan_group(row0, carry):
            a = a_ref[row0:row0 + SUBLANES_F32, :]
            b = b_ref[row0:row0 + SUBLANES_F32, :]
            for dist in (1, 2, 4):
                keep = rows >= dist
                a_prev = jnp.where(keep, pltpu.roll(a, dist, 0), 1.0)
                b_prev = jnp.where(keep, pltpu.roll(b, dist, 0), 0.0)
                b = a * b_prev + b
                a = a * a_prev
            hh = b + a * carry
            gate = drain_ref[tail + row0:tail + row0 + SUBLANES_F32, 0:r]
            y = hh * jax.nn.gelu(gate)
            return y, jnp.broadcast_to(hh[SUBLANES_F32 - 1:, :], (SUBLANES_F32, r))

        state = {"carry": jnp.where(drain_starts_seq, 0.0, carry_ref[...])}

        def scan_pair(row0):
            y0, carry = scan_group(row0, state["carry"])
            y1, carry = scan_group(row0 + SUBLANES_F32, carry)
            y_ref[row0:row0 + SUBLANES_BF16, :] = jnp.concatenate([y0, y1], axis=0).astype(BF16)
            state["carry"] = carry

        vector_tasks = [functools.partial(gates, hd) for hd in range(LRU_HEADS)]
        vector_tasks += [functools.partial(scan_pair, row0) for row0 in range(0, tm, SUBLANES_BF16)]
        per_chunk = -(-len(vector_tasks) // LRU_PROJ_CHUNKS)
        for c in range(LRU_PROJ_CHUNKS):
            project(c)
            for task in vector_tasks[c * per_chunk:(c + 1) * per_chunk]:
                task()

        fill_ref[0:tail, r:] = jnp.where(fill_starts_seq, 0.0, drain_ref[tm:tm + tail, r:])
        carry_ref[...] = state["carry"]
        o_ref[...] = hlag_ref[...] + jnp.dot(y_ref[...], wout_ref[...], preferred_element_type=F32)

    @pl.when(s % 2 == 0)
    def _():
        step(proj0_ref, proj1_ref)

    @pl.when(s % 2 == 1)
    def _():
        step(proj1_ref, proj0_ref)


def _lru(h, norms, w_in, conv_w, conv_b, w_ax, b_a, b_x, lam, w_out, layer, j, seq_len):
    m, d = h.shape
    tm = LRU_TOKEN_BLOCK
    n_blocks = m // tm
    r = w_out.shape[0]
    conv_width = conv_w.shape[1]
    heads, hd_dim = w_ax.shape[1], w_ax.shape[2]
    vec = lambda idx: pl.BlockSpec((None, 1, r), lambda s: (idx, 0, 0))
    lagged = lambda s: (jnp.maximum(s - 1, 0), 0)
    proj = pltpu.VMEM((SUBLANES_F32 + tm, 2 * r), F32)
    return pl.pallas_call(
        functools.partial(_lru_kernel, blocks_per_seq=seq_len // tm, conv_width=conv_width),
        grid=(n_blocks + 1,),
        in_specs=[
            pl.BlockSpec((tm, d), lambda s: (jnp.minimum(s, n_blocks - 1), 0)),
            pl.BlockSpec((tm, d), lagged),
            pl.BlockSpec((None, 1, d), lambda s: (layer, 0, 0)),
            _resident((d, 2 * r), lambda s: (0, 0)),
            pl.BlockSpec((None, conv_width, r), lambda s: (j, 0, 0)),
            vec(j),
            _resident((None, heads, hd_dim, 2 * hd_dim), lambda s: (j, 0, 0, 0)),
            vec(j),
            vec(j),
            vec(j),
            _resident((r, d), lambda s: (0, 0)),
        ],
        out_specs=pl.BlockSpec((tm, d), lagged),
        out_shape=jax.ShapeDtypeStruct((m, d), F32),
        scratch_shapes=[
            proj,
            proj,
            pltpu.VMEM((tm, r), F32),
            pltpu.VMEM((tm, r), F32),
            pltpu.VMEM((tm, r), BF16),
            pltpu.VMEM((SUBLANES_F32, r), F32),
        ],
        compiler_params=_params(("arbitrary",)),
        name=f"lru_l{layer}",
    )(h, h, norms, w_in, conv_w, conv_b, w_ax, b_a, b_x, lam, w_out)


def _kv_kernel(m_ref, g_ref, wk_ref, wv_ref, k_ref, v_ref):
    mn = _rmsnorm(m_ref[...], g_ref[...]).astype(BF16)
    k_ref[...] = jnp.dot(mn, wk_ref[...].astype(BF16), preferred_element_type=F32).astype(BF16)
    v_ref[...] = jnp.dot(mn, wv_ref[...].astype(BF16), preferred_element_type=F32).astype(BF16)


def _kv(mem2d, mem_norm, w_k, w_v):
    rows, d = mem2d.shape
    depth = w_k.shape[0]
    tn = KV_COLUMN_BLOCK
    out = jax.ShapeDtypeStruct((depth, rows, d), BF16)
    return pl.pallas_call(
        _kv_kernel,
        grid=(depth, d // tn),
        in_specs=[
            _resident((rows, d), lambda l, n: (0, 0)),
            pl.BlockSpec((1, d), lambda l, n: (0, 0)),
            pl.BlockSpec((None, d, tn), lambda l, n: (l, 0, n)),
            pl.BlockSpec((None, d, tn), lambda l, n: (l, 0, n)),
        ],
        out_specs=[pl.BlockSpec((None, rows, tn), lambda l, n: (l, 0, n))] * 2,
        out_shape=[out, out],
        compiler_params=_params(("arbitrary", "arbitrary")),
        name="kv_proj",
    )(mem2d, mem_norm, w_k, w_v)


def _xattn_kernel(*refs, n_casts):
    h_ref, g_ref, wq_ref, k_ref, v_ref, wo_ref = refs[:6]
    cast_in = refs[6:6 + n_casts]
    o_ref = refs[6 + n_casts]
    cast_out = refs[7 + n_casts:7 + 2 * n_casts]
    q_ref, a_ref = refs[7 + 2 * n_casts:]
    d = h_ref.shape[1]
    hd_dim = d // XATTN_HEADS
    scale = hd_dim ** -0.5
    x = h_ref[...]
    u = _rmsnorm(x, g_ref[...]).astype(BF16)
    q_ref[...] = jnp.dot(u, wq_ref[...], preferred_element_type=F32).astype(BF16)
    for hd in range(XATTN_HEADS):
        cols = slice(hd * hd_dim, (hd + 1) * hd_dim)
        s = lax.dot_general(q_ref[:, cols], k_ref[:, cols], (((1,), (1,)), ((), ())),
                            preferred_element_type=F32) * scale
        e = jnp.exp(s - jnp.max(s, axis=-1, keepdims=True))
        p = (e / jnp.sum(e, axis=-1, keepdims=True)).astype(BF16)
        a_ref[:, cols] = jnp.dot(p, v_ref[:, cols], preferred_element_type=F32).astype(BF16)
    o_ref[...] = x + jnp.dot(a_ref[...], wo_ref[...], preferred_element_type=F32)
    for src, dst in zip(cast_in, cast_out):
        dst[...] = src[...].astype(BF16)


def _xattn(h, norm, w_q, k_all, v_all, w_o, layer, seq_len, mem_len, casts):
    m, d = h.shape
    tm = XATTN_TOKEN_BLOCK
    n_blocks = m // tm
    blocks_per_seq = seq_len // tm
    in_specs = [
        pl.BlockSpec((tm, d), lambda i: (i, 0)),
        pl.BlockSpec((None, 1, d), lambda i: (layer, 0, 0)),
        _resident((d, d), lambda i: (0, 0)),
        pl.BlockSpec((None, mem_len, d), lambda i: (layer, i // blocks_per_seq, 0)),
        pl.BlockSpec((None, mem_len, d), lambda i: (layer, i // blocks_per_seq, 0)),
        _resident((d, d), lambda i: (0, 0)),
    ]
    out_specs = [pl.BlockSpec((tm, d), lambda i: (i, 0))]
    out_shape = [jax.ShapeDtypeStruct((m, d), F32)]
    args = [h, norm, w_q, k_all, v_all, w_o]
    for stacked, idx in casts:
        _, rows, cols = stacked.shape
        slab = rows // n_blocks
        in_specs.append(pl.BlockSpec((None, slab, cols), lambda i, idx=idx: (idx, i, 0)))
        out_specs.append(pl.BlockSpec((slab, cols), lambda i: (i, 0)))
        out_shape.append(jax.ShapeDtypeStruct((rows, cols), BF16))
        args.append(stacked)
    outs = pl.pallas_call(
        functools.partial(_xattn_kernel, n_casts=len(casts)),
        grid=(n_blocks,),
        in_specs=in_specs,
        out_specs=out_specs,
        out_shape=out_shape,
        scratch_shapes=[pltpu.VMEM((tm, d), BF16), pltpu.VMEM((tm, d), BF16)],
        compiler_params=_params(("arbitrary",)),
        name=f"xattn_l{layer}",
    )(*args)
    return outs[0], list(outs[1:])


def kernel(x, mem, ffn_norm, w_ffn_gate, w_ffn_up, w_ffn_down, mix_norm, pool_w, pool_scale, lru_w_in, lru_conv_w, lru_conv_b, lru_w_a, lru_b_a, lru_w_x, lru_b_x, lru_lambda, lru_w_out, xattn_norm, mem_norm, w_q, w_k, w_v, w_o, final_norm):
    batch, seq_len, d = x.shape
    mem_len = mem.shape[1]
    depth = ffn_norm.shape[0]
    n_mixers = 2

    row = lambda a: a.reshape(-1, 1, a.shape[-1])
    ffn_norm_r, mix_norm_r, xattn_norm_r = row(ffn_norm), row(mix_norm), row(xattn_norm)
    pool_scale_r = row(pool_scale)
    conv_b_r, b_a_r, b_x_r, lam_r = row(lru_conv_b), row(lru_b_a), row(lru_b_x), row(lru_lambda)
    final_g = final_norm.reshape(1, d)

    ffn_w = (w_ffn_gate[0, 0].astype(BF16), w_ffn_up[0, 0].astype(BF16), w_ffn_down[0, 0].astype(BF16))
    pool_w_b = pool_w.astype(BF16)
    w_ax = jnp.concatenate([lru_w_a, lru_w_x], axis=-1).astype(BF16)
    attn_w = (w_q[0].astype(BF16), w_o[0].astype(BF16))
    lru_w = None

    k_all, v_all = _kv(mem.reshape(batch * mem_len, d), mem_norm.reshape(1, d), w_k, w_v)

    def ffn(h, ffn_w, layer, half):
        last = layer == depth - 1 and half == 1
        nxt = None if last else (w_ffn_gate, w_ffn_up, w_ffn_down) + ((layer, 1) if half == 0 else (layer + 1, 0))
        outs = _ffn(h, ffn_norm_r, layer * 2 + half, *ffn_w, final_g, last, nxt, f"ffn_l{layer}_{half}")
        return (outs, None) if last else (outs[0], tuple(outs[1:]))

    h = x.reshape(batch * seq_len, d)
    for i in range(depth):
        j = i // n_mixers
        h, ffn_w = ffn(h, ffn_w, i, 0)
        if i % n_mixers == 0:
            h = _pool(h, mix_norm_r, pool_w_b, pool_scale_r, i, j, seq_len)
        else:
            h = _lru(h, mix_norm_r, lru_w[0], lru_conv_w, conv_b_r, w_ax, b_a_r, b_x_r, lam_r, lru_w[1],
                     i, j, seq_len)
        casts = []
        if i + 1 < depth:
            casts += [(w_q, i + 1), (w_o, i + 1)]
            if (i + 1) % n_mixers == 1:
                casts += [(lru_w_in, (i + 1) // n_mixers), (lru_w_out, (i + 1) // n_mixers)]
        h, cast = _xattn(h, xattn_norm_r, attn_w[0], k_all, v_all, attn_w[1], i, seq_len, mem_len, casts)
        if i + 1 < depth:
            attn_w = tuple(cast[:2])
            if (i + 1) % n_mixers == 1:
                lru_w = tuple(cast[2:])
        h, ffn_w = ffn(h, ffn_w, i, 1)
    return h.reshape(batch, seq_len, d)
```

```python
import functools

import jax
import jax.numpy as jnp
from jax import lax
from jax.experimental import pallas as pl
from jax.experimental.pallas import tpu as pltpu

F32 = jnp.float32
BF16 = jnp.bfloat16

EPS = 1e-6
MACARON_WEIGHT = 0.5
POOL_WINDOWS = (2, 4, 8, 16)
LRU_HEADS = 16
LRU_C = 8.0
XATTN_HEADS = 4

SUBLANES_F32 = 8
SUBLANES_BF16 = 16
VMEM_LIMIT_BYTES = 58 * 1024 * 1024

FFN_TOKEN_BLOCK = 1024
FFN_HIDDEN_BLOCK = 512
POOL_TOKEN_BLOCK = 512
LRU_TOKEN_BLOCK = 256
LRU_PROJ_CHUNKS = 8
XATTN_TOKEN_BLOCK = 512
KV_COLUMN_BLOCK = 512


def _rmsnorm(x, g):
    ms = jnp.mean(x * x, axis=-1, keepdims=True)
    return x * lax.rsqrt(ms + EPS) * g


def _params(semantics):
    return pltpu.CompilerParams(dimension_semantics=semantics, vmem_limit_bytes=VMEM_LIMIT_BYTES)


def _resident(block_shape, index_map):
    return pl.BlockSpec(block_shape, index_map, pipeline_mode=pl.Buffered(1))


def _ffn_kernel(*refs, apply_final_norm, cast_next):
    h_ref, g_ref, wg_ref, wu_ref, wd_ref, fg_ref = refs[:6]
    if cast_next:
        ng_ref, nu_ref, nd_ref, o_ref, ng_out, nu_out, nd_out, u_ref = refs[6:]
    else:
        o_ref, u_ref = refs[6:]
    j = pl.program_id(1)

    @pl.when(j == 0)
    def _():
        x = h_ref[...]
        u_ref[...] = _rmsnorm(x, g_ref[...]).astype(BF16)
        o_ref[...] = x

    u = u_ref[...]
    gate = jnp.dot(u, wg_ref[...], preferred_element_type=F32)
    up = jnp.dot(u, wu_ref[...], preferred_element_type=F32)
    act = (jax.nn.silu(gate) * up * MACARON_WEIGHT).astype(BF16)
    o_ref[...] += jnp.dot(act, wd_ref[...], preferred_element_type=F32)

    if cast_next:
        ng_out[...] = ng_ref[...].astype(BF16)
        nu_out[...] = nu_ref[...].astype(BF16)
        nd_out[...] = nd_ref[...].astype(BF16)

    if apply_final_norm:

        @pl.when(j == pl.num_programs(1) - 1)
        def _():
            o_ref[...] = _rmsnorm(o_ref[...], fg_ref[...])


def _ffn(h, norms, norm_row, wg, wu, wd, final_g, apply_final_norm, nxt, name):
    m, d = h.shape
    f = wg.shape[-1]
    tm, tf = FFN_TOKEN_BLOCK, FFN_HIDDEN_BLOCK
    n_i = m // tm
    in_specs = [
        pl.BlockSpec((tm, d), lambda i, j: (i, 0)),
        pl.BlockSpec((None, 1, d), lambda i, j: (norm_row, 0, 0)),
        pl.BlockSpec((d, tf), lambda i, j: (0, j)),
        pl.BlockSpec((d, tf), lambda i, j: (0, j)),
        pl.BlockSpec((tf, d), lambda i, j: (j, 0)),
        pl.BlockSpec((1, d), lambda i, j: (0, 0)),
    ]
    out_specs = [pl.BlockSpec((tm, d), lambda i, j: (i, 0))]
    out_shape = [jax.ShapeDtypeStruct((m, d), F32)]
    args = [h, norms, wg, wu, wd, final_g]
    if nxt is not None:
        n_gate, n_up, n_down, n_layer, n_half = nxt
        n_j = f // tf
        slab = f // (n_i * n_j)
        as_rows = lambda w: w.reshape(w.shape[0], w.shape[1], f, d)
        in_specs += [pl.BlockSpec((None, None, slab, d), lambda i, j: (n_layer, n_half, i * n_j + j, 0))] * 3
        out_specs += [pl.BlockSpec((slab, d), lambda i, j: (i * n_j + j, 0))] * 3
        out_shape += [jax.ShapeDtypeStruct((f, d), BF16)] * 3
        args += [as_rows(n_gate), as_rows(n_up), n_down]
    outs = pl.pallas_call(
        functools.partial(_ffn_kernel, apply_final_norm=apply_final_norm, cast_next=nxt is not None),
        grid=(n_i, f // tf),
        in_specs=in_specs,
        out_specs=out_specs,
        out_shape=out_shape,
        scratch_shapes=[pltpu.VMEM((tm, d), BF16)],
        compiler_params=_params(("parallel", "arbitrary")),
        name=name,
    )(*args)
    if nxt is None:
        return outs[0]
    h_new, ng, nu, nd = outs
    return h_new, ng.reshape(d, f), nu.reshape(d, f), nd


def _pool_kernel(h_ref, g_ref, w_ref, sc_ref, o_ref, buf_ref, tmp0_ref, tmp1_ref, *, blocks_per_seq, halo):
    tm = h_ref.shape[0]
    groups = len(POOL_WINDOWS)
    gd = h_ref.shape[1] // groups
    pad = 2 * halo
    rows = pad + tm
    blk = pl.program_id(0) % blocks_per_seq

    @pl.when(pl.program_id(0) == 0)
    def _():
        buf_ref[0:halo, :] = jnp.zeros((halo, buf_ref.shape[1]), F32)
        tmp0_ref[...] = jnp.zeros(tmp0_ref.shape, F32)
        tmp1_ref[...] = jnp.zeros(tmp1_ref.shape, F32)

    @pl.when(blk == 0)
    def _():
        buf_ref[halo:pad, :] = jnp.zeros((halo, buf_ref.shape[1]), F32)

    x = h_ref[...]
    buf_ref[pad:, :] = _rmsnorm(x, g_ref[...])
    pos = (blk * tm + 1 + lax.broadcasted_iota(jnp.int32, (tm, 1), 0)).astype(F32)

    for g, w in enumerate(POOL_WINDOWS):
        cols = slice(g * gd, (g + 1) * gd)
        src, src_cols = buf_ref, cols
        shift = 1
        for dst in (tmp0_ref, tmp1_ref, tmp0_ref, tmp1_ref):
            if shift >= w:
                break
            dst[halo:, :] = src[halo:rows, src_cols] + src[halo - shift:rows - shift, src_cols]
            src, src_cols = dst, slice(None)
            shift *= 2
        count = jnp.minimum(pos, float(w))
        pooled = (src[pad:, src_cols] / count - buf_ref[pad:, cols]).astype(BF16)
        y = jnp.dot(pooled, w_ref[g], preferred_element_type=F32)
        o_ref[:, cols] = x[:, cols] + y * sc_ref[:, cols]

    buf_ref[halo:pad, :] = buf_ref[tm + halo:tm + pad, :]


def _pool(h, norms, pool_w, pool_scale, layer, j, seq_len):
    m, d = h.shape
    tm = POOL_TOKEN_BLOCK
    halo = max(POOL_WINDOWS)
    groups, gd = pool_w.shape[1], pool_w.shape[2]
    return pl.pallas_call(
        functools.partial(_pool_kernel, blocks_per_seq=seq_len // tm, halo=halo),
        grid=(m // tm,),
        in_specs=[
            pl.BlockSpec((tm, d), lambda i: (i, 0)),
            pl.BlockSpec((None, 1, d), lambda i: (layer, 0, 0)),
            _resident((None, groups, gd, gd), lambda i: (j, 0, 0, 0)),
            pl.BlockSpec((None, 1, d), lambda i: (j, 0, 0)),
        ],
        out_specs=pl.BlockSpec((tm, d), lambda i: (i, 0)),
        out_shape=jax.ShapeDtypeStruct((m, d), F32),
        scratch_shapes=[
            pltpu.VMEM((2 * halo + tm, d), F32),
            pltpu.VMEM((2 * halo + tm, gd), F32),
            pltpu.VMEM((2 * halo + tm, gd), F32),
        ],
        compiler_params=_params(("arbitrary",)),
        name=f"pool_l{layer}",
    )(h, norms, pool_w, pool_scale)


def _softplus(z):
    return jnp.maximum(z, 0.0) + jnp.log1p(jnp.exp(-jnp.abs(z)))


def _sigmoid(z):
    return 0.5 * jnp.tanh(0.5 * z) + 0.5


def _lru_kernel(h_ref, hlag_ref, g_ref, win_ref, cw_ref, cb_ref, wax_ref, ba_ref, bx_ref, lam_ref, wout_ref,
                o_ref, proj0_ref, proj1_ref, a_ref, b_ref, y_ref, carry_ref, *, blocks_per_seq, conv_width):
    tm = h_ref.shape[0]
    r = a_ref.shape[1]
    hd_dim = r // LRU_HEADS
    tail = SUBLANES_F32
    s = pl.program_id(0)
    drain_starts_seq = (s + blocks_per_seq - 1) % blocks_per_seq == 0
    fill_starts_seq = s % blocks_per_seq == 0

    @pl.when(s == 0)
    def _():
        proj0_ref[...] = jnp.zeros(proj0_ref.shape, F32)
        proj1_ref[...] = jnp.zeros(proj1_ref.shape, F32)
        carry_ref[...] = jnp.zeros(carry_ref.shape, F32)

    def step(fill_ref, drain_ref):
        u = _rmsnorm(h_ref[...], g_ref[...]).astype(BF16)
        decay = _softplus(-lam_ref[...])
        rows = lax.broadcasted_iota(jnp.int32, (SUBLANES_F32, r), 0)
        chunk = 2 * r // LRU_PROJ_CHUNKS

        def project(c):
            cols = slice(c * chunk, (c + 1) * chunk)
            fill_ref[tail:, cols] = jnp.dot(u, win_ref[:, cols], preferred_element_type=F32)

        def gates(hd):
            cols = slice(hd * hd_dim, (hd + 1) * hd_dim)
            xcols = slice(r + hd * hd_dim, r + (hd + 1) * hd_dim)
            xc = cb_ref[:, cols]
            for k in range(conv_width):
                back = conv_width - 1 - k
                xc = xc + drain_ref[tail - back:tail - back + tm, xcols] * cw_ref[k:k + 1, cols]
            ra = jnp.dot(xc.astype(BF16), wax_ref[hd], preferred_element_type=F32)
            rg = _sigmoid(ra[:, :hd_dim] + ba_ref[:, cols])
            ig = _sigmoid(ra[:, hd_dim:] + bx_ref[:, cols])
            a = jnp.exp((-LRU_C) * rg * decay[:, cols])
            v = 1.0 - a * a
            a_ref[:, cols] = a
            b_ref[:, cols] = jnp.where(v > 0.0, v * lax.rsqrt(v), 0.0) * ig * xc

        def scan_group(row0, carry):
            a = a_ref[row0:row0 + SUBLANES_F32, :]
            b = b_ref[row0:row0 + SUBLANES_F32, :]
            for dist in (1, 2, 4):
                keep = rows >= dist
                a_prev = jnp.where(keep, pltpu.roll(a, dist, 0), 1.0)
                b_prev = jnp.where(keep, pltpu.roll(b, dist, 0), 0.0)
                b = a * b_prev + b
                a = a * a_prev
            hh = b + a * carry
            gate = drain_ref[tail + row0:tail + row0 + SUBLANES_F32, 0:r]
            y = hh * jax.nn.gelu(gate)
            return y, jnp.broadcast_to(hh[SUBLANES_F32 - 1:, :], (SUBLANES_F32, r))

        state = {"carry": jnp.where(drain_starts_seq, 0.0, carry_ref[...])}

        def scan_pair(row0):
            y0, carry = scan_group(row0, state["carry"])
            y1, carry = scan_group(row0 + SUBLANES_F32, carry)
            y_ref[row0:row0 + SUBLANES_BF16, :] = jnp.concatenate([y0, y1], axis=0).astype(BF16)
            state["carry"] = carry

        vector_tasks = [functools.partial(gates, hd) for hd in range(LRU_HEADS)]
        vector_tasks += [functools.partial(scan_pair, row0) for row0 in range(0, tm, SUBLANES_BF16)]
        per_chunk = -(-len(vector_tasks) // LRU_PROJ_CHUNKS)
        for c in range(LRU_PROJ_CHUNKS):
            project(c)
            for task in vector_tasks[c * per_chunk:(c + 1) * per_chunk]:
                task()

        fill_ref[0:tail, r:] = jnp.where(fill_starts_seq, 0.0, drain_ref[tm:tm + tail, r:])
        carry_ref[...] = state["carry"]
        o_ref[...] = hlag_ref[...] + jnp.dot(y_ref[...], wout_ref[...], preferred_element_type=F32)

    @pl.when(s % 2 == 0)
    def _():
        step(proj0_ref, proj1_ref)

    @pl.when(s % 2 == 1)
    def _():
        step(proj1_ref, proj0_ref)


def _lru(h, norms, w_in, conv_w, conv_b, w_ax, b_a, b_x, lam, w_out, layer, j, seq_len):
    m, d = h.shape
    tm = LRU_TOKEN_BLOCK
    n_blocks = m // tm
    r = w_out.shape[0]
    conv_width = conv_w.shape[1]
    heads, hd_dim = w_ax.shape[1], w_ax.shape[2]
    vec = lambda idx: pl.BlockSpec((None, 1, r), lambda s: (idx, 0, 0))
    lagged = lambda s: (jnp.maximum(s - 1, 0), 0)
    proj = pltpu.VMEM((SUBLANES_F32 + tm, 2 * r), F32)
    return pl.pallas_call(
        functools.partial(_lru_kernel, blocks_per_seq=seq_len // tm, conv_width=conv_width),
        grid=(n_blocks + 1,),
        in_specs=[
            pl.BlockSpec((tm, d), lambda s: (jnp.minimum(s, n_blocks - 1), 0)),
            pl.BlockSpec((tm, d), lagged),
            pl.BlockSpec((None, 1, d), lambda s: (layer, 0, 0)),
            _resident((d, 2 * r), lambda s: (0, 0)),
            pl.BlockSpec((None, conv_width, r), lambda s: (j, 0, 0)),
            vec(j),
            _resident((None, heads, hd_dim, 2 * hd_dim), lambda s: (j, 0, 0, 0)),
            vec(j),
            vec(j),
            vec(j),
            _resident((r, d), lambda s: (0, 0)),
        ],
        out_specs=pl.BlockSpec((tm, d), lagged),
        out_shape=jax.ShapeDtypeStruct((m, d), F32),
        scratch_shapes=[
            proj,
            proj,
            pltpu.VMEM((tm, r), F32),
            pltpu.VMEM((tm, r), F32),
            pltpu.VMEM((tm, r), BF16),
            pltpu.VMEM((SUBLANES_F32, r), F32),
        ],
        compiler_params=_params(("arbitrary",)),
        name=f"lru_l{layer}",
    )(h, h, norms, w_in, conv_w, conv_b, w_ax, b_a, b_x, lam, w_out)


def _kv_kernel(m_ref, g_ref, wk_ref, wv_ref, k_ref, v_ref):
    mn = _rmsnorm(m_ref[...], g_ref[...]).astype(BF16)
    k_ref[...] = jnp.dot(mn, wk_ref[...].astype(BF16), preferred_element_type=F32).astype(BF16)
    v_ref[...] = jnp.dot(mn, wv_ref[...].astype(BF16), preferred_element_type=F32).astype(BF16)


def _kv(mem2d, mem_norm, w_k, w_v):
    rows, d = mem2d.shape
    depth = w_k.shape[0]
    tn = KV_COLUMN_BLOCK
    out = jax.ShapeDtypeStruct((depth, rows, d), BF16)
    return pl.pallas_call(
        _kv_kernel,
        grid=(depth, d // tn),
        in_specs=[
            _resident((rows, d), lambda l, n: (0, 0)),
            pl.BlockSpec((1, d), lambda l, n: (0, 0)),
            pl.BlockSpec((None, d, tn), lambda l, n: (l, 0, n)),
            pl.BlockSpec((None, d, tn), lambda l, n: (l, 0, n)),
        ],
        out_specs=[pl.BlockSpec((None, rows, tn), lambda l, n: (l, 0, n))] * 2,
        out_shape=[out, out],
        compiler_params=_params(("arbitrary", "arbitrary")),
        name="kv_proj",
    )(mem2d, mem_norm, w_k, w_v)


def _xattn_kernel(*refs, n_casts):
    h_ref, g_ref, wq_ref, k_ref, v_ref, wo_ref = refs[:6]
    cast_in = refs[6:6 + n_casts]
    o_ref = refs[6 + n_casts]
    cast_out = refs[7 + n_casts:7 + 2 * n_casts]
    q_ref, a_ref = refs[7 + 2 * n_casts:]
    d = h_ref.shape[1]
    hd_dim = d // XATTN_HEADS
    scale = hd_dim ** -0.5
    x = h_ref[...]
    u = _rmsnorm(x, g_ref[...]).astype(BF16)
    q_ref[...] = jnp.dot(u, wq_ref[...], preferred_element_type=F32).astype(BF16)
    for hd in range(XATTN_HEADS):
        cols = slice(hd * hd_dim, (hd + 1) * hd_dim)
        s = lax.dot_general(q_ref[:, cols], k_ref[:, cols], (((1,), (1,)), ((), ())),
                            preferred_element_type=F32) * scale
        e = jnp.exp(s - jnp.max(s, axis=-1, keepdims=True))
        p = (e / jnp.sum(e, axis=-1, keepdims=True)).astype(BF16)
        a_ref[:, cols] = jnp.dot(p, v_ref[:, cols], preferred_element_type=F32).astype(BF16)
    o_ref[...] = x + jnp.dot(a_ref[...], wo_ref[...], preferred_element_type=F32)
    for src, dst in zip(cast_in, cast_out):
        dst[...] = src[...].astype(BF16)


def _xattn(h, norm, w_q, k_all, v_all, w_o, layer, seq_len, mem_len, casts):
    m, d = h.shape
    tm = XATTN_TOKEN_BLOCK
    n_blocks = m // tm
    blocks_per_seq = seq_len // tm
    in_specs = [
        pl.BlockSpec((tm, d), lambda i: (i, 0)),
        pl.BlockSpec((None, 1, d), lambda i: (layer, 0, 0)),
        _resident((d, d), lambda i: (0, 0)),
        pl.BlockSpec((None, mem_len, d), lambda i: (layer, i // blocks_per_seq, 0)),
        pl.BlockSpec((None, mem_len, d), lambda i: (layer, i // blocks_per_seq, 0)),
        _resident((d, d), lambda i: (0, 0)),
    ]
    out_specs = [pl.BlockSpec((tm, d), lambda i: (i, 0))]
    out_shape = [jax.ShapeDtypeStruct((m, d), F32)]
    args = [h, norm, w_q, k_all, v_all, w_o]
    for stacked, idx in casts:
        _, rows, cols = stacked.shape
        slab = rows // n_blocks
        in_specs.append(pl.BlockSpec((None, slab, cols), lambda i, idx=idx: (idx, i, 0)))
        out_specs.append(pl.BlockSpec((slab, cols), lambda i: (i, 0)))
        out_shape.append(jax.ShapeDtypeStruct((rows, cols), BF16))
        args.append(stacked)
    outs = pl.pallas_call(
        functools.partial(_xattn_kernel, n_casts=len(casts)),
        grid=(n_blocks,),
        in_specs=in_specs,
        out_specs=out_specs,
        out_shape=out_shape,
        scratch_shapes=[pltpu.VMEM((tm, d), BF16), pltpu.VMEM((tm, d), BF16)],
        compiler_params=_params(("arbitrary",)),
        name=f"xattn_l{layer}",
    )(*args)
    return outs[0], list(outs[1:])


def kernel(x, mem, ffn_norm, w_ffn_gate, w_ffn_up, w_ffn_down, mix_norm, pool_w, pool_scale, lru_w_in, lru_conv_w, lru_conv_b, lru_w_a, lru_b_a, lru_w_x, lru_b_x, lru_lambda, lru_w_out, xattn_norm, mem_norm, w_q, w_k, w_v, w_o, final_norm):
    batch, seq_len, d = x.shape
    mem_len = mem.shape[1]
    depth = ffn_norm.shape[0]
    n_mixers = 2

    row = lambda a: a.reshape(-1, 1, a.shape[-1])
    ffn_norm_r, mix_norm_r, xattn_norm_r = row(ffn_norm), row(mix_norm), row(xattn_norm)
    pool_scale_r = row(pool_scale)
    conv_b_r, b_a_r, b_x_r, lam_r = row(lru_conv_b), row(lru_b_a), row(lru_b_x), row(lru_lambda)
    final_g = final_norm.reshape(1, d)

    ffn_w = (w_ffn_gate[0, 0].astype(BF16), w_ffn_up[0, 0].astype(BF16), w_ffn_down[0, 0].astype(BF16))
    pool_w_b = pool_w.astype(BF16)
    w_ax = jnp.concatenate([lru_w_a, lru_w_x], axis=-1).astype(BF16)
    attn_w = (w_q[0].astype(BF16), w_o[0].astype(BF16))
    lru_w = None

    k_all, v_all = _kv(mem.reshape(batch * mem_len, d), mem_norm.reshape(1, d), w_k, w_v)

    def ffn(h, ffn_w, layer, half):
        last = layer == depth - 1 and half == 1
        nxt = None if last else (w_ffn_gate, w_ffn_up, w_ffn_down) + ((layer, 1) if half == 0 else (layer + 1, 0))
        outs = _ffn(h, ffn_norm_r, layer * 2 + half, *ffn_w, final_g, last, nxt, f"ffn_l{layer}_{half}")
        return (outs, None) if last else (outs[0], tuple(outs[1:]))

    h = x.reshape(batch * seq_len, d)
    for i in range(depth):
        j = i // n_mixers
        h, ffn_w = ffn(h, ffn_w, i, 0)
        if i % n_mixers == 0:
            h = _pool(h, mix_norm_r, pool_w_b, pool_scale_r, i, j, seq_len)
        else:
            h = _lru(h, mix_norm_r, lru_w[0], lru_conv_w, conv_b_r, w_ax, b_a_r, b_x_r, lam_r, lru_w[1],
                     i, j, seq_len)
        casts = []
        if i + 1 < depth:
            casts += [(w_q, i + 1), (w_o, i + 1)]
            if (i + 1) % n_mixers == 1:
                casts += [(lru_w_in, (i + 1) // n_mixers), (lru_w_out, (i + 1) // n_mixers)]
        h, cast = _xattn(h, xattn_norm_r, attn_w[0], k_all, v_all, attn_w[1], i, seq_len, mem_len, casts)
        if i + 1 < depth:
            attn_w = tuple(cast[:2])
            if (i + 1) % n_mixers == 1:
                lru_w = tuple(cast[2:])
        h, ffn_w = ffn(h, ffn_w, i, 1)
    return h.reshape(batch, seq_len, d)
```

```python
import functools

import jax
import jax.numpy as jnp
from jax import lax
from jax.experimental import pallas as pl
from jax.experimental.pallas import tpu as pltpu

F32 = jnp.float32
BF16 = jnp.bfloat16

EPS = 1e-6
MACARON_WEIGHT = 0.5
POOL_WINDOWS = (2, 4, 8, 16)
LRU_HEADS = 16
LRU_C = 8.0
XATTN_HEADS = 4

SUBLANES_F32 = 8
SUBLANES_BF16 = 16
VMEM_LIMIT_BYTES = 58 * 1024 * 1024

FFN_TOKEN_BLOCK = 1024
FFN_HIDDEN_BLOCK = 512
POOL_TOKEN_BLOCK = 512
LRU_TOKEN_BLOCK = 256
LRU_PROJ_CHUNKS = 8
XATTN_TOKEN_BLOCK = 512
KV_COLUMN_BLOCK = 512


def _rmsnorm(x, g):
    ms = jnp.mean(x * x, axis=-1, keepdims=True)
    return x * lax.rsqrt(ms + EPS) * g


def _params(semantics):
    return pltpu.CompilerParams(dimension_semantics=semantics, vmem_limit_bytes=VMEM_LIMIT_BYTES)


def _resident(block_shape, index_map):
    return pl.BlockSpec(block_shape, index_map, pipeline_mode=pl.Buffered(1))


def _ffn_kernel(*refs, apply_final_norm, cast_next):
    h_ref, g_ref, wg_ref, wu_ref, wd_ref, fg_ref = refs[:6]
    if cast_next:
        ng_ref, nu_ref, nd_ref, o_ref, ng_out, nu_out, nd_out, u_ref = refs[6:]
    else:
        o_ref, u_ref = refs[6:]
    j = pl.program_id(1)

    @pl.when(j == 0)
    def _():
        x = h_ref[...]
        u_ref[...] = _rmsnorm(x, g_ref[...]).astype(BF16)
        o_ref[...] = x

    u = u_ref[...]
    gate = jnp.dot(u, wg_ref[...], preferred_element_type=F32)
    up = jnp.dot(u, wu_ref[...], preferred_element_type=F32)
    act = (jax.nn.silu(gate) * up * MACARON_WEIGHT).astype(BF16)
    o_ref[...] += jnp.dot(act, wd_ref[...], preferred_element_type=F32)

    if cast_next:
        ng_out[...] = ng_ref[...].astype(BF16)
        nu_out[...] = nu_ref[...].astype(BF16)
        nd_out[...] = nd_ref[...].astype(BF16)

    if apply_final_norm:

        @pl.when(j == pl.num_programs(1) - 1)
        def _():
            o_ref[...] = _rmsnorm(o_ref[...], fg_ref[...])


def _ffn(h, norms, norm_row, wg, wu, wd, final_g, apply_final_norm, nxt, name):
    m, d = h.shape
    tm, tf = FFN_TOKEN_BLOCK, FFN_HIDDEN_BLOCK
    f = wd.shape[0]
    n_i = m // tm
    in_specs = [
        pl.BlockSpec((tm, d), lambda i, j: (i, 0)),
        pl.BlockSpec((None, 1, d), lambda i, j: (norm_row, 0, 0)),
        pl.BlockSpec((None, d, tf), lambda i, j: (j, 0, 0)),
        pl.BlockSpec((None, d, tf), lambda i, j: (j, 0, 0)),
        pl.BlockSpec((tf, d), lambda i, j: (j, 0)),
        pl.BlockSpec((1, d), lambda i, j: (0, 0)),
    ]
    out_specs = [pl.BlockSpec((tm, d), lambda i, j: (i, 0))]
    out_shape = [jax.ShapeDtypeStruct((m, d), F32)]
    args = [h, norms, wg, wu, wd, final_g]
    if nxt is not None:
        n_gate, n_up, n_down, n_layer, n_half = nxt
        ds = d // n_i
        in_specs += [
            pl.BlockSpec((None, None, ds, tf), lambda i, j: (n_layer, n_half, i, j)),
            pl.BlockSpec((None, None, ds, tf), lambda i, j: (n_layer, n_half, i, j)),
            pl.BlockSpec((None, None, tf, ds), lambda i, j: (n_layer, n_half, j, i)),
        ]
        out_specs += [
            pl.BlockSpec((None, ds, tf), lambda i, j: (j, i, 0)),
            pl.BlockSpec((None, ds, tf), lambda i, j: (j, i, 0)),
            pl.BlockSpec((tf, ds), lambda i, j: (j, i)),
        ]
        tiled = jax.ShapeDtypeStruct((f // tf, d, tf), BF16)
        out_shape += [tiled, tiled, jax.ShapeDtypeStruct((f, d), BF16)]
        args += [n_gate, n_up, n_down]
    outs = pl.pallas_call(
        functools.partial(_ffn_kernel, apply_final_norm=apply_final_norm, cast_next=nxt is not None),
        grid=(n_i, f // tf),
        in_specs=in_specs,
        out_specs=out_specs,
        out_shape=out_shape,
        scratch_shapes=[pltpu.VMEM((tm, d), BF16)],
        compiler_params=_params(("parallel", "arbitrary")),
        name=name,
    )(*args)
    return outs if nxt is not None else outs[0]


def _pool_kernel(h_ref, g_ref, w_ref, sc_ref, o_ref, buf_ref, tmp0_ref, tmp1_ref, *, blocks_per_seq, halo):
    tm = h_ref.shape[0]
    groups = len(POOL_WINDOWS)
    gd = h_ref.shape[1] // groups
    pad = 2 * halo
    rows = pad + tm
    blk = pl.program_id(0) % blocks_per_seq

    @pl.when(pl.program_id(0) == 0)
    def _():
        buf_ref[0:halo, :] = jnp.zeros((halo, buf_ref.shape[1]), F32)
        tmp0_ref[...] = jnp.zeros(tmp0_ref.shape, F32)
        tmp1_ref[...] = jnp.zeros(tmp1_ref.shape, F32)

    @pl.when(blk == 0)
    def _():
        buf_ref[halo:pad, :] = jnp.zeros((halo, buf_ref.shape[1]), F32)

    x = h_ref[...]
    buf_ref[pad:, :] = _rmsnorm(x, g_ref[...])
    pos = (blk * tm + 1 + lax.broadcasted_iota(jnp.int32, (tm, 1), 0)).astype(F32)

    for g, w in enumerate(POOL_WINDOWS):
        cols = slice(g * gd, (g + 1) * gd)
        src, src_cols = buf_ref, cols
        shift = 1
        for dst in (tmp0_ref, tmp1_ref, tmp0_ref, tmp1_ref):
            if shift >= w:
                break
            dst[halo:, :] = src[halo:rows, src_cols] + src[halo - shift:rows - shift, src_cols]
            src, src_cols = dst, slice(None)
            shift *= 2
        count = jnp.minimum(pos, float(w))
        pooled = (src[pad:, src_cols] / count - buf_ref[pad:, cols]).astype(BF16)
        y = jnp.dot(pooled, w_ref[g], preferred_element_type=F32)
        o_ref[:, cols] = x[:, cols] + y * sc_ref[:, cols]

    buf_ref[halo:pad, :] = buf_ref[tm + halo:tm + pad, :]


def _pool(h, norms, pool_w, pool_scale, layer, j, seq_len):
    m, d = h.shape
    tm = POOL_TOKEN_BLOCK
    halo = max(POOL_WINDOWS)
    groups, gd = pool_w.shape[1], pool_w.shape[2]
    return pl.pallas_call(
        functools.partial(_pool_kernel, blocks_per_seq=seq_len // tm, halo=halo),
        grid=(m // tm,),
        in_specs=[
            pl.BlockSpec((tm, d), lambda i: (i, 0)),
            pl.BlockSpec((None, 1, d), lambda i: (layer, 0, 0)),
            _resident((None, groups, gd, gd), lambda i: (j, 0, 0, 0)),
            pl.BlockSpec((None, 1, d), lambda i: (j, 0, 0)),
        ],
        out_specs=pl.BlockSpec((tm, d), lambda i: (i, 0)),
        out_shape=jax.ShapeDtypeStruct((m, d), F32),
        scratch_shapes=[
            pltpu.VMEM((2 * halo + tm, d), F32),
            pltpu.VMEM((2 * halo + tm, gd), F32),
            pltpu.VMEM((2 * halo + tm, gd), F32),
        ],
        compiler_params=_params(("arbitrary",)),
        name=f"pool_l{layer}",
    )(h, norms, pool_w, pool_scale)


def _softplus(z):
    return jnp.maximum(z, 0.0) + jnp.log1p(jnp.exp(-jnp.abs(z)))


def _sigmoid(z):
    return 0.5 * jnp.tanh(0.5 * z) + 0.5


def _lru_kernel(h_ref, hlag_ref, g_ref, win_ref, cw_ref, cb_ref, wax_ref, ba_ref, bx_ref, lam_ref, wout_ref,
                o_ref, proj0_ref, proj1_ref, a_ref, b_ref, y_ref, carry_ref, *, blocks_per_seq, conv_width):
    tm = h_ref.shape[0]
    r = a_ref.shape[1]
    hd_dim = r // LRU_HEADS
    tail = SUBLANES_F32
    s = pl.program_id(0)
    drain_starts_seq = (s + blocks_per_seq - 1) % blocks_per_seq == 0
    fill_starts_seq = s % blocks_per_seq == 0

    @pl.when(s == 0)
    def _():
        proj0_ref[...] = jnp.zeros(proj0_ref.shape, F32)
        proj1_ref[...] = jnp.zeros(proj1_ref.shape, F32)
        carry_ref[...] = jnp.zeros(carry_ref.shape, F32)

    def step(fill_ref, drain_ref):
        u = _rmsnorm(h_ref[...], g_ref[...]).astype(BF16)
        decay = _softplus(-lam_ref[...])
        rows = lax.broadcasted_iota(jnp.int32, (SUBLANES_F32, r), 0)
        chunk = 2 * r // LRU_PROJ_CHUNKS

        def project(c):
            cols = slice(c * chunk, (c + 1) * chunk)
            fill_ref[tail:, cols] = jnp.dot(u, win_ref[:, cols], preferred_element_type=F32)

        def gates(hd):
            cols = slice(hd * hd_dim, (hd + 1) * hd_dim)
            xcols = slice(r + hd * hd_dim, r + (hd + 1) * hd_dim)
            xc = cb_ref[:, cols]
            for k in range(conv_width):
                back = conv_width - 1 - k
                xc = xc + drain_ref[tail - back:tail - back + tm, xcols] * cw_ref[k:k + 1, cols]
            ra = jnp.dot(xc.astype(BF16), wax_ref[hd], preferred_element_type=F32)
            rg = _sigmoid(ra[:, :hd_dim] + ba_ref[:, cols])
            ig = _sigmoid(ra[:, hd_dim:] + bx_ref[:, cols])
            a = jnp.exp((-LRU_C) * rg * decay[:, cols])
            v = 1.0 - a * a
            a_ref[:, cols] = a
            b_ref[:, cols] = jnp.where(v > 0.0, v * lax.rsqrt(v), 0.0) * ig * xc

        def scan_group(row0, carry):
            a = a_ref[row0:row0 + SUBLANES_F32, :]
            b = b_ref[row0:row0 + SUBLANES_F32, :]
            for dist in (1, 2, 4):
                keep = rows >= dist
                a_prev = jnp.where(keep, pltpu.roll(a, dist, 0), 1.0)
                b_prev = jnp.where(keep, pltpu.roll(b, dist, 0), 0.0)
                b = a * b_prev + b
                a = a * a_prev
            hh = b + a * carry
            gate = drain_ref[tail + row0:tail + row0 + SUBLANES_F32, 0:r]
            y = hh * jax.nn.gelu(gate)
            return y, jnp.broadcast_to(hh[SUBLANES_F32 - 1:, :], (SUBLANES_F32, r))

        state = {"carry": jnp.where(drain_starts_seq, 0.0, carry_ref[...])}

        def scan_pair(row0):
            y0, carry = scan_group(row0, state["carry"])
            y1, carry = scan_group(row0 + SUBLANES_F32, carry)
            y_ref[row0:row0 + SUBLANES_BF16, :] = jnp.concatenate([y0, y1], axis=0).astype(BF16)
            state["carry"] = carry

        vector_tasks = [functools.partial(gates, hd) for hd in range(LRU_HEADS)]
        vector_tasks += [functools.partial(scan_pair, row0) for row0 in range(0, tm, SUBLANES_BF16)]
        per_chunk = -(-len(vector_tasks) // LRU_PROJ_CHUNKS)
        for c in range(LRU_PROJ_CHUNKS):
            project(c)
            for task in vector_tasks[c * per_chunk:(c + 1) * per_chunk]:
                task()

        fill_ref[0:tail, r:] = jnp.where(fill_starts_seq, 0.0, drain_ref[tm:tm + tail, r:])
        carry_ref[...] = state["carry"]
        o_ref[...] = hlag_ref[...] + jnp.dot(y_ref[...], wout_ref[...], preferred_element_type=F32)

    @pl.when(s % 2 == 0)
    def _():
        step(proj0_ref, proj1_ref)

    @pl.when(s % 2 == 1)
    def _():
        step(proj1_ref, proj0_ref)


def _lru(h, norms, w_in, conv_w, conv_b, w_ax, b_a, b_x, lam, w_out, layer, j, seq_len):
    m, d = h.shape
    tm = LRU_TOKEN_BLOCK
    n_blocks = m // tm
    r = w_out.shape[0]
    conv_width = conv_w.shape[1]
    heads, hd_dim = w_ax.shape[1], w_ax.shape[2]
    vec = lambda idx: pl.BlockSpec((None, 1, r), lambda s: (idx, 0, 0))
    lagged = lambda s: (jnp.maximum(s - 1, 0), 0)
    proj = pltpu.VMEM((SUBLANES_F32 + tm, 2 * r), F32)
    return pl.pallas_call(
        functools.partial(_lru_kernel, blocks_per_seq=seq_len // tm, conv_width=conv_width),
        grid=(n_blocks + 1,),
        in_specs=[
            pl.BlockSpec((tm, d), lambda s: (jnp.minimum(s, n_blocks - 1), 0)),
            pl.BlockSpec((tm, d), lagged),
            pl.BlockSpec((None, 1, d), lambda s: (layer, 0, 0)),
            _resident((d, 2 * r), lambda s: (0, 0)),
            pl.BlockSpec((None, conv_width, r), lambda s: (j, 0, 0)),
            vec(j),
            _resident((None, heads, hd_dim, 2 * hd_dim), lambda s: (j, 0, 0, 0)),
            vec(j),
            vec(j),
            vec(j),
            _resident((r, d), lambda s: (0, 0)),
        ],
        out_specs=pl.BlockSpec((tm, d), lagged),
        out_shape=jax.ShapeDtypeStruct((m, d), F32),
        scratch_shapes=[
            proj,
            proj,
            pltpu.VMEM((tm, r), F32),
            pltpu.VMEM((tm, r), F32),
            pltpu.VMEM((tm, r), BF16),
            pltpu.VMEM((SUBLANES_F32, r), F32),
        ],
        compiler_params=_params(("arbitrary",)),
        name=f"lru_l{layer}",
    )(h, h, norms, w_in, conv_w, conv_b, w_ax, b_a, b_x, lam, w_out)


def _kv_kernel(m_ref, g_ref, wk_ref, wv_ref, k_ref, v_ref):
    mn = _rmsnorm(m_ref[...], g_ref[...]).astype(BF16)
    k_ref[...] = jnp.dot(mn, wk_ref[...].astype(BF16), preferred_element_type=F32).astype(BF16)
    v_ref[...] = jnp.dot(mn, wv_ref[...].astype(BF16), preferred_element_type=F32).astype(BF16)


def _kv(mem2d, mem_norm, w_k, w_v):
    rows, d = mem2d.shape
    depth = w_k.shape[0]
    tn = KV_COLUMN_BLOCK
    out = jax.ShapeDtypeStruct((depth, rows, d), BF16)
    return pl.pallas_call(
        _kv_kernel,
        grid=(depth, d // tn),
        in_specs=[
            _resident((rows, d), lambda l, n: (0, 0)),
            pl.BlockSpec((1, d), lambda l, n: (0, 0)),
            pl.BlockSpec((None, d, tn), lambda l, n: (l, 0, n)),
            pl.BlockSpec((None, d, tn), lambda l, n: (l, 0, n)),
        ],
        out_specs=[pl.BlockSpec((None, rows, tn), lambda l, n: (l, 0, n))] * 2,
        out_shape=[out, out],
        compiler_params=_params(("arbitrary", "arbitrary")),
        name="kv_proj",
    )(mem2d, mem_norm, w_k, w_v)


def _xattn_kernel(*refs, n_casts):
    h_ref, g_ref, wq_ref, k_ref, v_ref, wo_ref = refs[:6]
    cast_in = refs[6:6 + n_casts]
    o_ref = refs[6 + n_casts]
    cast_out = refs[7 + n_casts:7 + 2 * n_casts]
    q_ref, a_ref = refs[7 + 2 * n_casts:]
    d = h_ref.shape[1]
    hd_dim = d // XATTN_HEADS
    scale = hd_dim ** -0.5
    x = h_ref[...]
    u = _rmsnorm(x, g_ref[...]).astype(BF16)
    q_ref[...] = jnp.dot(u, wq_ref[...], preferred_element_type=F32).astype(BF16)
    for hd in range(XATTN_HEADS):
        cols = slice(hd * hd_dim, (hd + 1) * hd_dim)
        s = lax.dot_general(q_ref[:, cols], k_ref[:, cols], (((1,), (1,)), ((), ())),
                            preferred_element_type=F32) * scale
        e = jnp.exp(s - jnp.max(s, axis=-1, keepdims=True))
        p = (e / jnp.sum(e, axis=-1, keepdims=True)).astype(BF16)
        a_ref[:, cols] = jnp.dot(p, v_ref[:, cols], preferred_element_type=F32).astype(BF16)
    o_ref[...] = x + jnp.dot(a_ref[...], wo_ref[...], preferred_element_type=F32)
    for src, dst in zip(cast_in, cast_out):
        dst[...] = src[...].astype(BF16)


def _xattn(h, norm, w_q, k_all, v_all, w_o, layer, seq_len, mem_len, casts):
    m, d = h.shape
    tm = XATTN_TOKEN_BLOCK
    n_blocks = m // tm
    blocks_per_seq = seq_len // tm
    in_specs = [
        pl.BlockSpec((tm, d), lambda i: (i, 0)),
        pl.BlockSpec((None, 1, d), lambda i: (layer, 0, 0)),
        _resident((d, d), lambda i: (0, 0)),
        pl.BlockSpec((None, mem_len, d), lambda i: (layer, i // blocks_per_seq, 0)),
        pl.BlockSpec((None, mem_len, d), lambda i: (layer, i // blocks_per_seq, 0)),
        _resident((d, d), lambda i: (0, 0)),
    ]
    out_specs = [pl.BlockSpec((tm, d), lambda i: (i, 0))]
    out_shape = [jax.ShapeDtypeStruct((m, d), F32)]
    args = [h, norm, w_q, k_all, v_all, w_o]
    for stacked, idx in casts:
        _, rows, cols = stacked.shape
        slab = rows // n_blocks
        in_specs.append(pl.BlockSpec((None, slab, cols), lambda i, idx=idx: (idx, i, 0)))
        out_specs.append(pl.BlockSpec((slab, cols), lambda i: (i, 0)))
        out_shape.append(jax.ShapeDtypeStruct((rows, cols), BF16))
        args.append(stacked)
    outs = pl.pallas_call(
        functools.partial(_xattn_kernel, n_casts=len(casts)),
        grid=(n_blocks,),
        in_specs=in_specs,
        out_specs=out_specs,
        out_shape=out_shape,
        scratch_shapes=[pltpu.VMEM((tm, d), BF16), pltpu.VMEM((tm, d), BF16)],
        compiler_params=_params(("arbitrary",)),
        name=f"xattn_l{layer}",
    )(*args)
    return outs[0], list(outs[1:])


def kernel(x, mem, ffn_norm, w_ffn_gate, w_ffn_up, w_ffn_down, mix_norm, pool_w, pool_scale, lru_w_in, lru_conv_w, lru_conv_b, lru_w_a, lru_b_a, lru_w_x, lru_b_x, lru_lambda, lru_w_out, xattn_norm, mem_norm, w_q, w_k, w_v, w_o, final_norm):
    batch, seq_len, d = x.shape
    mem_len = mem.shape[1]
    depth = ffn_norm.shape[0]
    n_mixers = 2

    row = lambda a: a.reshape(-1, 1, a.shape[-1])
    ffn_norm_r, mix_norm_r, xattn_norm_r = row(ffn_norm), row(mix_norm), row(xattn_norm)
    pool_scale_r = row(pool_scale)
    conv_b_r, b_a_r, b_x_r, lam_r = row(lru_conv_b), row(lru_b_a), row(lru_b_x), row(lru_lambda)
    final_g = final_norm.reshape(1, d)

    column_tiles = lambda w: w.astype(BF16).reshape(d, -1, FFN_HIDDEN_BLOCK).transpose(1, 0, 2)
    ffn_w = (column_tiles(w_ffn_gate[0, 0]), column_tiles(w_ffn_up[0, 0]), w_ffn_down[0, 0].astype(BF16))
    pool_w_b = pool_w.astype(BF16)
    w_ax = jnp.concatenate([lru_w_a, lru_w_x], axis=-1).astype(BF16)
    attn_w = (w_q[0].astype(BF16), w_o[0].astype(BF16))
    lru_w = None

    k_all, v_all = _kv(mem.reshape(batch * mem_len, d), mem_norm.reshape(1, d), w_k, w_v)

    def ffn(h, ffn_w, layer, half):
        last = layer == depth - 1 and half == 1
        nxt = None if last else (w_ffn_gate, w_ffn_up, w_ffn_down) + ((layer, 1) if half == 0 else (layer + 1, 0))
        outs = _ffn(h, ffn_norm_r, layer * 2 + half, *ffn_w, final_g, last, nxt, f"ffn_l{layer}_{half}")
        return (outs, None) if last else (outs[0], tuple(outs[1:]))

    h = x.reshape(batch * seq_len, d)
    for i in range(depth):
        j = i // n_mixers
        h, ffn_w = ffn(h, ffn_w, i, 0)
        if i % n_mixers == 0:
            h = _pool(h, mix_norm_r, pool_w_b, pool_scale_r, i, j, seq_len)
        else:
            h = _lru(h, mix_norm_r, lru_w[0], lru_conv_w, conv_b_r, w_ax, b_a_r, b_x_r, lam_r, lru_w[1],
                     i, j, seq_len)
        casts = []
        if i + 1 < depth:
            casts += [(w_q, i + 1), (w_o, i + 1)]
            if (i + 1) % n_mixers == 1:
                casts += [(lru_w_in, (i + 1) // n_mixers), (lru_w_out, (i + 1) // n_mixers)]
        h, cast = _xattn(h, xattn_norm_r, attn_w[0], k_all, v_all, attn_w[1], i, seq_len, mem_len, casts)
        if i + 1 < depth:
            attn_w = tuple(cast[:2])
            if (i + 1) % n_mixers == 1:
                lru_w = tuple(cast[2:])
        h, ffn_w = ffn(h, ffn_w, i, 1)
    return h.reshape(batch, seq_len, d)
```

```python
import functools

import jax
import jax.numpy as jnp
from jax import lax
from jax.experimental import pallas as pl
from jax.experimental.pallas import tpu as pltpu

F32 = jnp.float32
BF16 = jnp.bfloat16

EPS = 1e-6
MACARON_WEIGHT = 0.5
POOL_WINDOWS = (2, 4, 8, 16)
LRU_HEADS = 16
LRU_C = 8.0
XATTN_HEADS = 4

SUBLANES_F32 = 8
SUBLANES_BF16 = 16
VMEM_LIMIT_BYTES = 58 * 1024 * 1024

FFN_TOKEN_BLOCK = 1024
FFN_HIDDEN_BLOCK = 512
POOL_TOKEN_BLOCK = 512
LRU_TOKEN_BLOCK = 256
LRU_PROJ_CHUNKS = 8
XATTN_TOKEN_BLOCK = 512
KV_COLUMN_BLOCK = 256


def _rmsnorm(x, g):
    ms = jnp.mean(x * x, axis=-1, keepdims=True)
    return x * lax.rsqrt(ms + EPS) * g


def _params(semantics):
    return pltpu.CompilerParams(dimension_semantics=semantics, vmem_limit_bytes=VMEM_LIMIT_BYTES)


def _resident(block_shape, index_map):
    return pl.BlockSpec(block_shape, index_map, pipeline_mode=pl.Buffered(1))


def _ffn_kernel(*refs, apply_final_norm, cast_next):
    h_ref, g_ref, wg_ref, wu_ref, wd_ref, fg_ref = refs[:6]
    if cast_next:
        ng_ref, nu_ref, nd_ref, o_ref, ng_out, nu_out, nd_out, u_ref = refs[6:]
    else:
        o_ref, u_ref = refs[6:]
    j = pl.program_id(1)

    @pl.when(j == 0)
    def _():
        x = h_ref[...]
        u_ref[...] = _rmsnorm(x, g_ref[...]).astype(BF16)
        o_ref[...] = x

    u = u_ref[...]
    gate = jnp.dot(u, wg_ref[...], preferred_element_type=F32)
    up = jnp.dot(u, wu_ref[...], preferred_element_type=F32)
    act = (jax.nn.silu(gate) * up * MACARON_WEIGHT).astype(BF16)
    o_ref[...] += jnp.dot(act, wd_ref[...], preferred_element_type=F32)

    if cast_next:
        ng_out[...] = ng_ref[...].astype(BF16)
        nu_out[...] = nu_ref[...].astype(BF16)
        nd_out[...] = nd_ref[...].astype(BF16)

    if apply_final_norm:

        @pl.when(j == pl.num_programs(1) - 1)
        def _():
            o_ref[...] = _rmsnorm(o_ref[...], fg_ref[...])


def _ffn(h, norms, norm_row, wg, wu, wd, final_g, apply_final_norm, nxt, name):
    m, d = h.shape
    tm, tf = FFN_TOKEN_BLOCK, FFN_HIDDEN_BLOCK
    f = wd.shape[0]
    n_i = m // tm
    in_specs = [
        pl.BlockSpec((tm, d), lambda i, j: (i, 0)),
        pl.BlockSpec((None, 1, d), lambda i, j: (norm_row, 0, 0)),
        pl.BlockSpec((None, d, tf), lambda i, j: (j, 0, 0)),
        pl.BlockSpec((None, d, tf), lambda i, j: (j, 0, 0)),
        pl.BlockSpec((tf, d), lambda i, j: (j, 0)),
        pl.BlockSpec((1, d), lambda i, j: (0, 0)),
    ]
    out_specs = [pl.BlockSpec((tm, d), lambda i, j: (i, 0))]
    out_shape = [jax.ShapeDtypeStruct((m, d), F32)]
    args = [h, norms, wg, wu, wd, final_g]
    if nxt is not None:
        n_gate, n_up, n_down, n_layer, n_half = nxt
        ds = d // n_i
        in_specs += [
            pl.BlockSpec((None, None, ds, tf), lambda i, j: (n_layer, n_half, i, j)),
            pl.BlockSpec((None, None, ds, tf), lambda i, j: (n_layer, n_half, i, j)),
            pl.BlockSpec((None, None, tf, ds), lambda i, j: (n_layer, n_half, j, i)),
        ]
        out_specs += [
            pl.BlockSpec((None, ds, tf), lambda i, j: (j, i, 0)),
            pl.BlockSpec((None, ds, tf), lambda i, j: (j, i, 0)),
            pl.BlockSpec((tf, ds), lambda i, j: (j, i)),
        ]
        tiled = jax.ShapeDtypeStruct((f // tf, d, tf), BF16)
        out_shape += [tiled, tiled, jax.ShapeDtypeStruct((f, d), BF16)]
        args += [n_gate, n_up, n_down]
    outs = pl.pallas_call(
        functools.partial(_ffn_kernel, apply_final_norm=apply_final_norm, cast_next=nxt is not None),
        grid=(n_i, f // tf),
        in_specs=in_specs,
        out_specs=out_specs,
        out_shape=out_shape,
        scratch_shapes=[pltpu.VMEM((tm, d), BF16)],
        compiler_params=_params(("parallel", "arbitrary")),
        name=name,
    )(*args)
    return outs if nxt is not None else outs[0]


def _pool_kernel(h_ref, g_ref, w_ref, sc_ref, o_ref, buf_ref, tmp0_ref, tmp1_ref, *, blocks_per_seq, halo):
    tm = h_ref.shape[0]
    groups = len(POOL_WINDOWS)
    gd = h_ref.shape[1] // groups
    pad = 2 * halo
    rows = pad + tm
    blk = pl.program_id(0) % blocks_per_seq

    @pl.when(pl.program_id(0) == 0)
    def _():
        buf_ref[0:halo, :] = jnp.zeros((halo, buf_ref.shape[1]), F32)
        tmp0_ref[...] = jnp.zeros(tmp0_ref.shape, F32)
        tmp1_ref[...] = jnp.zeros(tmp1_ref.shape, F32)

    @pl.when(blk == 0)
    def _():
        buf_ref[halo:pad, :] = jnp.zeros((halo, buf_ref.shape[1]), F32)

    x = h_ref[...]
    buf_ref[pad:, :] = _rmsnorm(x, g_ref[...])
    pos = (blk * tm + 1 + lax.broadcasted_iota(jnp.int32, (tm, 1), 0)).astype(F32)

    for g, w in enumerate(POOL_WINDOWS):
        cols = slice(g * gd, (g + 1) * gd)
        src, src_cols = buf_ref, cols
        shift = 1
        for dst in (tmp0_ref, tmp1_ref, tmp0_ref, tmp1_ref):
            if shift >= w:
                break
            dst[halo:, :] = src[halo:rows, src_cols] + src[halo - shift:rows - shift, src_cols]
            src, src_cols = dst, slice(None)
            shift *= 2
        inv_count = 1.0 / jnp.minimum(pos, float(w))
        pooled = (src[pad:, src_cols] * inv_count - buf_ref[pad:, cols]).astype(BF16)
        y = jnp.dot(pooled, w_ref[g], preferred_element_type=F32)
        o_ref[:, cols] = x[:, cols] + y * sc_ref[:, cols]

    buf_ref[halo:pad, :] = buf_ref[tm + halo:tm + pad, :]


def _pool(h, norms, pool_w, pool_scale, layer, j, seq_len):
    m, d = h.shape
    tm = POOL_TOKEN_BLOCK
    halo = max(POOL_WINDOWS)
    groups, gd = pool_w.shape[1], pool_w.shape[2]
    return pl.pallas_call(
        functools.partial(_pool_kernel, blocks_per_seq=seq_len // tm, halo=halo),
        grid=(m // tm,),
        in_specs=[
            pl.BlockSpec((tm, d), lambda i: (i, 0)),
            pl.BlockSpec((None, 1, d), lambda i: (layer, 0, 0)),
            _resident((None, groups, gd, gd), lambda i: (j, 0, 0, 0)),
            pl.BlockSpec((None, 1, d), lambda i: (j, 0, 0)),
        ],
        out_specs=pl.BlockSpec((tm, d), lambda i: (i, 0)),
        out_shape=jax.ShapeDtypeStruct((m, d), F32),
        scratch_shapes=[
            pltpu.VMEM((2 * halo + tm, d), F32),
            pltpu.VMEM((2 * halo + tm, gd), F32),
            pltpu.VMEM((2 * halo + tm, gd), F32),
        ],
        compiler_params=_params(("arbitrary",)),
        name=f"pool_l{layer}",
    )(h, norms, pool_w, pool_scale)


def _softplus(z):
    return jnp.maximum(z, 0.0) + jnp.log1p(jnp.exp(-jnp.abs(z)))


def _sigmoid(z):
    return 0.5 * jnp.tanh(0.5 * z) + 0.5


def _lru_kernel(h_ref, hlag_ref, g_ref, win_ref, cw_ref, cb_ref, wax_ref, ba_ref, bx_ref, lam_ref, wout_ref,
                o_ref, proj0_ref, proj1_ref, a_ref, b_ref, y_ref, carry_ref, *, blocks_per_seq, conv_width):
    tm = h_ref.shape[0]
    r = a_ref.shape[1]
    hd_dim = r // LRU_HEADS
    tail = SUBLANES_F32
    s = pl.program_id(0)
    drain_starts_seq = (s + blocks_per_seq - 1) % blocks_per_seq == 0
    fill_starts_seq = s % blocks_per_seq == 0

    @pl.when(s == 0)
    def _():
        proj0_ref[...] = jnp.zeros(proj0_ref.shape, F32)
        proj1_ref[...] = jnp.zeros(proj1_ref.shape, F32)
        carry_ref[...] = jnp.zeros(carry_ref.shape, F32)

    def step(fill_ref, drain_ref):
        u = _rmsnorm(h_ref[...], g_ref[...]).astype(BF16)
        decay = _softplus(-lam_ref[...])
        rows = lax.broadcasted_iota(jnp.int32, (SUBLANES_F32, r), 0)
        chunk = 2 * r // LRU_PROJ_CHUNKS

        def project(c):
            cols = slice(c * chunk, (c + 1) * chunk)
            fill_ref[tail:, cols] = jnp.dot(u, win_ref[:, cols], preferred_element_type=F32)

        def gates(hd):
            cols = slice(hd * hd_dim, (hd + 1) * hd_dim)
            xcols = slice(r + hd * hd_dim, r + (hd + 1) * hd_dim)
            xc = cb_ref[:, cols]
            for k in range(conv_width):
                back = conv_width - 1 - k
                xc = xc + drain_ref[tail - back:tail - back + tm, xcols] * cw_ref[k:k + 1, cols]
            ra = jnp.dot(xc.astype(BF16), wax_ref[hd], preferred_element_type=F32)
            rg = _sigmoid(ra[:, :hd_dim] + ba_ref[:, cols])
            ig = _sigmoid(ra[:, hd_dim:] + bx_ref[:, cols])
            a = jnp.exp((-LRU_C) * rg * decay[:, cols])
            v = 1.0 - a * a
            a_ref[:, cols] = a
            b_ref[:, cols] = jnp.where(v > 0.0, v * lax.rsqrt(v), 0.0) * ig * xc

        def scan_group(row0, carry):
            a = a_ref[row0:row0 + SUBLANES_F32, :]
            b = b_ref[row0:row0 + SUBLANES_F32, :]
            for dist in (1, 2, 4):
                keep = rows >= dist
                a_prev = jnp.where(keep, pltpu.roll(a, dist, 0), 1.0)
                b_prev = jnp.where(keep, pltpu.roll(b, dist, 0), 0.0)
                b = a * b_prev + b
                a = a * a_prev
            hh = b + a * carry
            gate = drain_ref[tail + row0:tail + row0 + SUBLANES_F32, 0:r]
            y = hh * jax.nn.gelu(gate)
            return y, jnp.broadcast_to(hh[SUBLANES_F32 - 1:, :], (SUBLANES_F32, r))

        state = {"carry": jnp.where(drain_starts_seq, 0.0, carry_ref[...])}

        def scan_pair(row0):
            y0, carry = scan_group(row0, state["carry"])
            y1, carry = scan_group(row0 + SUBLANES_F32, carry)
            y_ref[row0:row0 + SUBLANES_BF16, :] = jnp.concatenate([y0, y1], axis=0).astype(BF16)
            state["carry"] = carry

        vector_tasks = [functools.partial(gates, hd) for hd in range(LRU_HEADS)]
        vector_tasks += [functools.partial(scan_pair, row0) for row0 in range(0, tm, SUBLANES_BF16)]
        per_chunk = -(-len(vector_tasks) // LRU_PROJ_CHUNKS)
        for c in range(LRU_PROJ_CHUNKS):
            project(c)
            for task in vector_tasks[c * per_chunk:(c + 1) * per_chunk]:
                task()

        fill_ref[0:tail, r:] = jnp.where(fill_starts_seq, 0.0, drain_ref[tm:tm + tail, r:])
        carry_ref[...] = state["carry"]
        o_ref[...] = hlag_ref[...] + jnp.dot(y_ref[...], wout_ref[...], preferred_element_type=F32)

    @pl.when(s % 2 == 0)
    def _():
        step(proj0_ref, proj1_ref)

    @pl.when(s % 2 == 1)
    def _():
        step(proj1_ref, proj0_ref)


def _lru(h, norms, w_in, conv_w, conv_b, w_ax, b_a, b_x, lam, w_out, layer, j, seq_len):
    m, d = h.shape
    tm = LRU_TOKEN_BLOCK
    n_blocks = m // tm
    r = w_out.shape[0]
    conv_width = conv_w.shape[1]
    heads, hd_dim = w_ax.shape[1], w_ax.shape[2]
    vec = lambda idx: pl.BlockSpec((None, 1, r), lambda s: (idx, 0, 0))
    lagged = lambda s: (jnp.maximum(s - 1, 0), 0)
    proj = pltpu.VMEM((SUBLANES_F32 + tm, 2 * r), F32)
    return pl.pallas_call(
        functools.partial(_lru_kernel, blocks_per_seq=seq_len // tm, conv_width=conv_width),
        grid=(n_blocks + 1,),
        in_specs=[
            pl.BlockSpec((tm, d), lambda s: (jnp.minimum(s, n_blocks - 1), 0)),
            pl.BlockSpec((tm, d), lagged),
            pl.BlockSpec((None, 1, d), lambda s: (layer, 0, 0)),
            _resident((d, 2 * r), lambda s: (0, 0)),
            pl.BlockSpec((None, conv_width, r), lambda s: (j, 0, 0)),
            vec(j),
            _resident((None, heads, hd_dim, 2 * hd_dim), lambda s: (j, 0, 0, 0)),
            vec(j),
            vec(j),
            vec(j),
            _resident((r, d), lambda s: (0, 0)),
        ],
        out_specs=pl.BlockSpec((tm, d), lagged),
        out_shape=jax.ShapeDtypeStruct((m, d), F32),
        scratch_shapes=[
            proj,
            proj,
            pltpu.VMEM((tm, r), F32),
            pltpu.VMEM((tm, r), F32),
            pltpu.VMEM((tm, r), BF16),
            pltpu.VMEM((SUBLANES_F32, r), F32),
        ],
        compiler_params=_params(("arbitrary",)),
        name=f"lru_l{layer}",
    )(h, h, norms, w_in, conv_w, conv_b, w_ax, b_a, b_x, lam, w_out)


def _kv_kernel(m_ref, g_ref, wk_ref, wv_ref, fg_ref, fu_ref, fd_ref, wq_ref, wo_ref,
               k_out, v_out, fg_out, fu_out, fd_out, wq_out, wo_out, mn_ref):
    first = jnp.logical_and(pl.program_id(0) == 0, pl.program_id(1) == 0)

    @pl.when(first)
    def _():
        mn_ref[...] = _rmsnorm(m_ref[...], g_ref[...]).astype(BF16)

    mn = mn_ref[...]
    k_out[...] = jnp.dot(mn, wk_ref[...].astype(BF16), preferred_element_type=F32).astype(BF16)
    v_out[...] = jnp.dot(mn, wv_ref[...].astype(BF16), preferred_element_type=F32).astype(BF16)

    tf = fg_out.shape[2]
    for j in range(fg_out.shape[0]):
        fg_out[j] = fg_ref[:, j * tf:(j + 1) * tf].astype(BF16)
        fu_out[j] = fu_ref[:, j * tf:(j + 1) * tf].astype(BF16)
    fd_out[...] = fd_ref[...].astype(BF16)
    wq_out[...] = wq_ref[...].astype(BF16)
    wo_out[...] = wo_ref[...].astype(BF16)


def _kv(mem2d, mem_norm, w_k, w_v, w_gate, w_up, w_down, w_q, w_o):
    rows, d = mem2d.shape
    depth = w_k.shape[0]
    f = w_down.shape[2]
    tn, tf = KV_COLUMN_BLOCK, FFN_HIDDEN_BLOCK
    n_n = d // tn
    steps = depth * n_n
    step = lambda l, n: l * n_n + n
    kv_out = jax.ShapeDtypeStruct((depth, rows, d), BF16)
    outs = pl.pallas_call(
        _kv_kernel,
        grid=(depth, n_n),
        in_specs=[
            _resident((rows, d), lambda l, n: (0, 0)),
            pl.BlockSpec((1, d), lambda l, n: (0, 0)),
            pl.BlockSpec((None, d, tn), lambda l, n: (l, 0, n)),
            pl.BlockSpec((None, d, tn), lambda l, n: (l, 0, n)),
            pl.BlockSpec((None, None, d // steps, f), lambda l, n: (0, 0, step(l, n), 0)),
            pl.BlockSpec((None, None, d // steps, f), lambda l, n: (0, 0, step(l, n), 0)),
            pl.BlockSpec((None, None, f // steps, d), lambda l, n: (0, 0, step(l, n), 0)),
            pl.BlockSpec((None, d // steps, d), lambda l, n: (0, step(l, n), 0)),
            pl.BlockSpec((None, d // steps, d), lambda l, n: (0, step(l, n), 0)),
        ],
        out_specs=[
            pl.BlockSpec((None, rows, tn), lambda l, n: (l, 0, n)),
            pl.BlockSpec((None, rows, tn), lambda l, n: (l, 0, n)),
            pl.BlockSpec((f // tf, d // steps, tf), lambda l, n: (0, step(l, n), 0)),
            pl.BlockSpec((f // tf, d // steps, tf), lambda l, n: (0, step(l, n), 0)),
            pl.BlockSpec((f // steps, d), lambda l, n: (step(l, n), 0)),
            pl.BlockSpec((d // steps, d), lambda l, n: (step(l, n), 0)),
            pl.BlockSpec((d // steps, d), lambda l, n: (step(l, n), 0)),
        ],
        out_shape=[kv_out, kv_out,
                   jax.ShapeDtypeStruct((f // tf, d, tf), BF16), jax.ShapeDtypeStruct((f // tf, d, tf), BF16),
                   jax.ShapeDtypeStruct((f, d), BF16),
                   jax.ShapeDtypeStruct((d, d), BF16), jax.ShapeDtypeStruct((d, d), BF16)],
        scratch_shapes=[pltpu.VMEM((rows, d), BF16)],
        compiler_params=_params(("arbitrary", "arbitrary")),
        name="kv_proj",
    )(mem2d, mem_norm, w_k, w_v, w_gate, w_up, w_down, w_q, w_o)
    return outs[0], outs[1], tuple(outs[2:5]), tuple(outs[5:7])


def _xattn_kernel(*refs, n_casts):
    h_ref, g_ref, wq_ref, k_ref, v_ref, wo_ref = refs[:6]
    cast_in = refs[6:6 + n_casts]
    o_ref = refs[6 + n_casts]
    cast_out = refs[7 + n_casts:7 + 2 * n_casts]
    q_ref, a_ref = refs[7 + 2 * n_casts:]
    d = h_ref.shape[1]
    hd_dim = d // XATTN_HEADS
    scale = hd_dim ** -0.5
    x = h_ref[...]
    u = _rmsnorm(x, g_ref[...]).astype(BF16)
    q_ref[...] = jnp.dot(u, wq_ref[...], preferred_element_type=F32).astype(BF16)
    for hd in range(XATTN_HEADS):
        cols = slice(hd * hd_dim, (hd + 1) * hd_dim)
        s = lax.dot_general(q_ref[:, cols], k_ref[:, cols], (((1,), (1,)), ((), ())),
                            preferred_element_type=F32) * scale
        e = jnp.exp(s - jnp.max(s, axis=-1, keepdims=True))
        p = (e * (1.0 / jnp.sum(e, axis=-1, keepdims=True))).astype(BF16)
        a_ref[:, cols] = jnp.dot(p, v_ref[:, cols], preferred_element_type=F32).astype(BF16)
    o_ref[...] = x + jnp.dot(a_ref[...], wo_ref[...], preferred_element_type=F32)
    for src, dst in zip(cast_in, cast_out):
        dst[...] = src[...].astype(BF16)


def _xattn(h, norm, w_q, k_all, v_all, w_o, layer, seq_len, mem_len, casts):
    m, d = h.shape
    tm = XATTN_TOKEN_BLOCK
    n_blocks = m // tm
    blocks_per_seq = seq_len // tm
    in_specs = [
        pl.BlockSpec((tm, d), lambda i: (i, 0)),
        pl.BlockSpec((None, 1, d), lambda i: (layer, 0, 0)),
        _resident((d, d), lambda i: (0, 0)),
        pl.BlockSpec((None, mem_len, d), lambda i: (layer, i // blocks_per_seq, 0)),
        pl.BlockSpec((None, mem_len, d), lambda i: (layer, i // blocks_per_seq, 0)),
        _resident((d, d), lambda i: (0, 0)),
    ]
    out_specs = [pl.BlockSpec((tm, d), lambda i: (i, 0))]
    out_shape = [jax.ShapeDtypeStruct((m, d), F32)]
    args = [h, norm, w_q, k_all, v_all, w_o]
    for stacked, idx in casts:
        _, rows, cols = stacked.shape
        slab = rows // n_blocks
        in_specs.append(pl.BlockSpec((None, slab, cols), lambda i, idx=idx: (idx, i, 0)))
        out_specs.append(pl.BlockSpec((slab, cols), lambda i: (i, 0)))
        out_shape.append(jax.ShapeDtypeStruct((rows, cols), BF16))
        args.append(stacked)
    outs = pl.pallas_call(
        functools.partial(_xattn_kernel, n_casts=len(casts)),
        grid=(n_blocks,),
        in_specs=in_specs,
        out_specs=out_specs,
        out_shape=out_shape,
        scratch_shapes=[pltpu.VMEM((tm, d), BF16), pltpu.VMEM((tm, d), BF16)],
        compiler_params=_params(("arbitrary",)),
        name=f"xattn_l{layer}",
    )(*args)
    return outs[0], list(outs[1:])


def kernel(x, mem, ffn_norm, w_ffn_gate, w_ffn_up, w_ffn_down, mix_norm, pool_w, pool_scale, lru_w_in, lru_conv_w, lru_conv_b, lru_w_a, lru_b_a, lru_w_x, lru_b_x, lru_lambda, lru_w_out, xattn_norm, mem_norm, w_q, w_k, w_v, w_o, final_norm):
    batch, seq_len, d = x.shape
    mem_len = mem.shape[1]
    depth = ffn_norm.shape[0]
    n_mixers = 2

    row = lambda a: a.reshape(-1, 1, a.shape[-1])
    ffn_norm_r, mix_norm_r, xattn_norm_r = row(ffn_norm), row(mix_norm), row(xattn_norm)
    pool_scale_r = row(pool_scale)
    conv_b_r, b_a_r, b_x_r, lam_r = row(lru_conv_b), row(lru_b_a), row(lru_b_x), row(lru_lambda)
    final_g = final_norm.reshape(1, d)

    pool_w_b = pool_w.astype(BF16)
    w_ax = jnp.concatenate([lru_w_a, lru_w_x], axis=-1).astype(BF16)
    lru_w = None

    k_all, v_all, ffn_w, attn_w = _kv(mem.reshape(batch * mem_len, d), mem_norm.reshape(1, d), w_k, w_v,
                                      w_ffn_gate, w_ffn_up, w_ffn_down, w_q, w_o)

    def ffn(h, ffn_w, layer, half):
        last = layer == depth - 1 and half == 1
        nxt = None if last else (w_ffn_gate, w_ffn_up, w_ffn_down) + ((layer, 1) if half == 0 else (layer + 1, 0))
        outs = _ffn(h, ffn_norm_r, layer * 2 + half, *ffn_w, final_g, last, nxt, f"ffn_l{layer}_{half}")
        return (outs, None) if last else (outs[0], tuple(outs[1:]))

    h = x.reshape(batch * seq_len, d)
    for i in range(depth):
        j = i // n_mixers
        h, ffn_w = ffn(h, ffn_w, i, 0)
        if i % n_mixers == 0:
            h = _pool(h, mix_norm_r, pool_w_b, pool_scale_r, i, j, seq_len)
        else:
            h = _lru(h, mix_norm_r, lru_w[0], lru_conv_w, conv_b_r, w_ax, b_a_r, b_x_r, lam_r, lru_w[1],
                     i, j, seq_len)
        casts = []
        if i + 1 < depth:
            casts += [(w_q, i + 1), (w_o, i + 1)]
            if (i + 1) % n_mixers == 1:
                casts += [(lru_w_in, (i + 1) // n_mixers), (lru_w_out, (i + 1) // n_mixers)]
        h, cast = _xattn(h, xattn_norm_r, attn_w[0], k_all, v_all, attn_w[1], i, seq_len, mem_len, casts)
        if i + 1 < depth:
            attn_w = tuple(cast[:2])
            if (i + 1) % n_mixers == 1:
                lru_w = tuple(cast[2:])
        h, ffn_w = ffn(h, ffn_w, i, 1)
    return h.reshape(batch, seq_len, d)
```

```python
import functools

import jax
import jax.numpy as jnp
from jax import lax
from jax.experimental import pallas as pl
from jax.experimental.pallas import tpu as pltpu

F32 = jnp.float32
BF16 = jnp.bfloat16

EPS = 1e-6
MACARON_WEIGHT = 0.5
POOL_WINDOWS = (2, 4, 8, 16)
LRU_HEADS = 16
LRU_C = 8.0
XATTN_HEADS = 4

SUBLANES_F32 = 8
SUBLANES_BF16 = 16
VMEM_LIMIT_BYTES = 58 * 1024 * 1024

FFN_TOKEN_BLOCK = 1024
FFN_HIDDEN_BLOCK = 512
POOL_TOKEN_BLOCK = 512
LRU_TOKEN_BLOCK = 256
LRU_PROJ_CHUNKS = 16
XATTN_TOKEN_BLOCK = 512
KV_COLUMN_BLOCK = 256


def _rmsnorm(x, g):
    ms = jnp.mean(x * x, axis=-1, keepdims=True)
    return x * lax.rsqrt(ms + EPS) * g


def _params(semantics):
    return pltpu.CompilerParams(dimension_semantics=semantics, vmem_limit_bytes=VMEM_LIMIT_BYTES)


def _resident(block_shape, index_map):
    return pl.BlockSpec(block_shape, index_map, pipeline_mode=pl.Buffered(1))


def _ffn_kernel(*refs, apply_final_norm, cast_next):
    h_ref, g_ref, wg_ref, wu_ref, wd_ref, fg_ref = refs[:6]
    if cast_next:
        ng_ref, nu_ref, nd_ref, o_ref, ng_out, nu_out, nd_out, u_ref = refs[6:]
    else:
        o_ref, u_ref = refs[6:]
    j = pl.program_id(1)

    @pl.when(j == 0)
    def _():
        x = h_ref[...]
        u_ref[...] = _rmsnorm(x, g_ref[...]).astype(BF16)
        o_ref[...] = x

    u = u_ref[...]
    gate = jnp.dot(u, wg_ref[...], preferred_element_type=F32)
    up = jnp.dot(u, wu_ref[...], preferred_element_type=F32)
    act = (jax.nn.silu(gate) * up * MACARON_WEIGHT).astype(BF16)
    o_ref[...] += jnp.dot(act, wd_ref[...], preferred_element_type=F32)

    if cast_next:
        ng_out[...] = ng_ref[...].astype(BF16)
        nu_out[...] = nu_ref[...].astype(BF16)
        nd_out[...] = nd_ref[...].astype(BF16)

    if apply_final_norm:

        @pl.when(j == pl.num_programs(1) - 1)
        def _():
            o_ref[...] = _rmsnorm(o_ref[...], fg_ref[...])


def _ffn(h, norms, norm_row, wg, wu, wd, final_g, apply_final_norm, nxt, name):
    m, d = h.shape
    tm, tf = FFN_TOKEN_BLOCK, FFN_HIDDEN_BLOCK
    f = wd.shape[0]
    n_i = m // tm
    in_specs = [
        pl.BlockSpec((tm, d), lambda i, j: (i, 0)),
        pl.BlockSpec((None, 1, d), lambda i, j: (norm_row, 0, 0)),
        pl.BlockSpec((None, d, tf), lambda i, j: (j, 0, 0)),
        pl.BlockSpec((None, d, tf), lambda i, j: (j, 0, 0)),
        pl.BlockSpec((tf, d), lambda i, j: (j, 0)),
        pl.BlockSpec((1, d), lambda i, j: (0, 0)),
    ]
    out_specs = [pl.BlockSpec((tm, d), lambda i, j: (i, 0))]
    out_shape = [jax.ShapeDtypeStruct((m, d), F32)]
    args = [h, norms, wg, wu, wd, final_g]
    if nxt is not None:
        n_gate, n_up, n_down, n_layer, n_half = nxt
        ds = d // n_i
        in_specs += [
            pl.BlockSpec((None, None, ds, tf), lambda i, j: (n_layer, n_half, i, j)),
            pl.BlockSpec((None, None, ds, tf), lambda i, j: (n_layer, n_half, i, j)),
            pl.BlockSpec((None, None, tf, ds), lambda i, j: (n_layer, n_half, j, i)),
        ]
        out_specs += [
            pl.BlockSpec((None, ds, tf), lambda i, j: (j, i, 0)),
            pl.BlockSpec((None, ds, tf), lambda i, j: (j, i, 0)),
            pl.BlockSpec((tf, ds), lambda i, j: (j, i)),
        ]
        tiled = jax.ShapeDtypeStruct((f // tf, d, tf), BF16)
        out_shape += [tiled, tiled, jax.ShapeDtypeStruct((f, d), BF16)]
        args += [n_gate, n_up, n_down]
    outs = pl.pallas_call(
        functools.partial(_ffn_kernel, apply_final_norm=apply_final_norm, cast_next=nxt is not None),
        grid=(n_i, f // tf),
        in_specs=in_specs,
        out_specs=out_specs,
        out_shape=out_shape,
        scratch_shapes=[pltpu.VMEM((tm, d), BF16)],
        compiler_params=_params(("parallel", "arbitrary")),
        name=name,
    )(*args)
    return outs if nxt is not None else outs[0]


def _pool_kernel(h_ref, g_ref, w_ref, sc_ref, o_ref, buf_ref, tmp0_ref, tmp1_ref, *, blocks_per_seq, halo):
    tm = h_ref.shape[0]
    groups = len(POOL_WINDOWS)
    gd = h_ref.shape[1] // groups
    pad = 2 * halo
    rows = pad + tm
    blk = pl.program_id(0) % blocks_per_seq

    @pl.when(pl.program_id(0) == 0)
    def _():
        buf_ref[0:halo, :] = jnp.zeros((halo, buf_ref.shape[1]), F32)
        tmp0_ref[...] = jnp.zeros(tmp0_ref.shape, F32)
        tmp1_ref[...] = jnp.zeros(tmp1_ref.shape, F32)

    @pl.when(blk == 0)
    def _():
        buf_ref[halo:pad, :] = jnp.zeros((halo, buf_ref.shape[1]), F32)

    x = h_ref[...]
    buf_ref[pad:, :] = _rmsnorm(x, g_ref[...])
    pos = (blk * tm + 1 + lax.broadcasted_iota(jnp.int32, (tm, 1), 0)).astype(F32)

    for g, w in enumerate(POOL_WINDOWS):
        cols = slice(g * gd, (g + 1) * gd)
        src, src_cols = buf_ref, cols
        shift = 1
        for dst in (tmp0_ref, tmp1_ref, tmp0_ref, tmp1_ref):
            if shift >= w:
                break
            dst[halo:, :] = src[halo:rows, src_cols] + src[halo - shift:rows - shift, src_cols]
            src, src_cols = dst, slice(None)
            shift *= 2
        inv_count = 1.0 / jnp.minimum(pos, float(w))
        pooled = (src[pad:, src_cols] * inv_count - buf_ref[pad:, cols]).astype(BF16)
        y = jnp.dot(pooled, w_ref[g], preferred_element_type=F32)
        o_ref[:, cols] = x[:, cols] + y * sc_ref[:, cols]

    buf_ref[halo:pad, :] = buf_ref[tm + halo:tm + pad, :]


def _pool(h, norms, pool_w, pool_scale, layer, j, seq_len):
    m, d = h.shape
    tm = POOL_TOKEN_BLOCK
    halo = max(POOL_WINDOWS)
    groups, gd = pool_w.shape[1], pool_w.shape[2]
    return pl.pallas_call(
        functools.partial(_pool_kernel, blocks_per_seq=seq_len // tm, halo=halo),
        grid=(m // tm,),
        in_specs=[
            pl.BlockSpec((tm, d), lambda i: (i, 0)),
            pl.BlockSpec((None, 1, d), lambda i: (layer, 0, 0)),
            _resident((None, groups, gd, gd), lambda i: (j, 0, 0, 0)),
            pl.BlockSpec((None, 1, d), lambda i: (j, 0, 0)),
        ],
        out_specs=pl.BlockSpec((tm, d), lambda i: (i, 0)),
        out_shape=jax.ShapeDtypeStruct((m, d), F32),
        scratch_shapes=[
            pltpu.VMEM((2 * halo + tm, d), F32),
            pltpu.VMEM((2 * halo + tm, gd), F32),
            pltpu.VMEM((2 * halo + tm, gd), F32),
        ],
        compiler_params=_params(("arbitrary",)),
        name=f"pool_l{layer}",
    )(h, norms, pool_w, pool_scale)


def _softplus(z):
    return jnp.maximum(z, 0.0) + jnp.log1p(jnp.exp(-jnp.abs(z)))


def _sigmoid(z):
    return 0.5 * jnp.tanh(0.5 * z) + 0.5


def _lru_kernel(h_ref, hlag_ref, g_ref, win_ref, cw_ref, cb_ref, wax_ref, ba_ref, bx_ref, lam_ref, wout_ref,
                o_ref, proj0_ref, proj1_ref, a_ref, b_ref, y_ref, carry_ref, *, blocks_per_seq, conv_width):
    tm = h_ref.shape[0]
    r = a_ref.shape[1]
    hd_dim = r // LRU_HEADS
    tail = SUBLANES_F32
    s = pl.program_id(0)
    drain_starts_seq = (s + blocks_per_seq - 1) % blocks_per_seq == 0
    fill_starts_seq = s % blocks_per_seq == 0

    @pl.when(s == 0)
    def _():
        proj0_ref[...] = jnp.zeros(proj0_ref.shape, F32)
        proj1_ref[...] = jnp.zeros(proj1_ref.shape, F32)
        carry_ref[...] = jnp.zeros(carry_ref.shape, F32)

    def step(fill_ref, drain_ref):
        u = _rmsnorm(h_ref[...], g_ref[...]).astype(BF16)
        decay = _softplus(-lam_ref[...])
        rows = lax.broadcasted_iota(jnp.int32, (SUBLANES_F32, r), 0)
        chunk = 2 * r // LRU_PROJ_CHUNKS

        def project(c):
            cols = slice(c * chunk, (c + 1) * chunk)
            fill_ref[tail:, cols] = jnp.dot(u, win_ref[:, cols], preferred_element_type=F32)

        def gates(hd):
            cols = slice(hd * hd_dim, (hd + 1) * hd_dim)
            xcols = slice(r + hd * hd_dim, r + (hd + 1) * hd_dim)
            xc = cb_ref[:, cols]
            for k in range(conv_width):
                back = conv_width - 1 - k
                xc = xc + drain_ref[tail - back:tail - back + tm, xcols] * cw_ref[k:k + 1, cols]
            ra = jnp.dot(xc.astype(BF16), wax_ref[hd], preferred_element_type=F32)
            rg = _sigmoid(ra[:, :hd_dim] + ba_ref[:, cols])
            ig = _sigmoid(ra[:, hd_dim:] + bx_ref[:, cols])
            a = jnp.exp((-LRU_C) * rg * decay[:, cols])
            v = 1.0 - a * a
            a_ref[:, cols] = a
            b_ref[:, cols] = jnp.where(v > 0.0, v * lax.rsqrt(v), 0.0) * ig * xc

        def scan_group(row0, carry):
            a = a_ref[row0:row0 + SUBLANES_F32, :]
            b = b_ref[row0:row0 + SUBLANES_F32, :]
            for dist in (1, 2, 4):
                keep = rows >= dist
                a_prev = jnp.where(keep, pltpu.roll(a, dist, 0), 1.0)
                b_prev = jnp.where(keep, pltpu.roll(b, dist, 0), 0.0)
                b = a * b_prev + b
                a = a * a_prev
            hh = b + a * carry
            gate = drain_ref[tail + row0:tail + row0 + SUBLANES_F32, 0:r]
            y = hh * jax.nn.gelu(gate)
            return y, jnp.broadcast_to(hh[SUBLANES_F32 - 1:, :], (SUBLANES_F32, r))

        state = {"carry": jnp.where(drain_starts_seq, 0.0, carry_ref[...])}

        def scan_pair(row0):
            y0, carry = scan_group(row0, state["carry"])
            y1, carry = scan_group(row0 + SUBLANES_F32, carry)
            y_ref[row0:row0 + SUBLANES_BF16, :] = jnp.concatenate([y0, y1], axis=0).astype(BF16)
            state["carry"] = carry

        vector_tasks = [functools.partial(gates, hd) for hd in range(LRU_HEADS)]
        vector_tasks += [functools.partial(scan_pair, row0) for row0 in range(0, tm, SUBLANES_BF16)]
        per_chunk = -(-len(vector_tasks) // LRU_PROJ_CHUNKS)
        for c in range(LRU_PROJ_CHUNKS):
            project(c)
            for task in vector_tasks[c * per_chunk:(c + 1) * per_chunk]:
                task()

        fill_ref[0:tail, r:] = jnp.where(fill_starts_seq, 0.0, drain_ref[tm:tm + tail, r:])
        carry_ref[...] = state["carry"]
        o_ref[...] = hlag_ref[...] + jnp.dot(y_ref[...], wout_ref[...], preferred_element_type=F32)

    @pl.when(s % 2 == 0)
    def _():
        step(proj0_ref, proj1_ref)

    @pl.when(s % 2 == 1)
    def _():
        step(proj1_ref, proj0_ref)


def _lru(h, norms, w_in, conv_w, conv_b, w_ax, b_a, b_x, lam, w_out, layer, j, seq_len):
    m, d = h.shape
    tm = LRU_TOKEN_BLOCK
    n_blocks = m // tm
    r = w_out.shape[0]
    conv_width = conv_w.shape[1]
    heads, hd_dim = w_ax.shape[1], w_ax.shape[2]
    vec = lambda idx: pl.BlockSpec((None, 1, r), lambda s: (idx, 0, 0))
    lagged = lambda s: (jnp.maximum(s - 1, 0), 0)
    proj = pltpu.VMEM((SUBLANES_F32 + tm, 2 * r), F32)
    return pl.pallas_call(
        functools.partial(_lru_kernel, blocks_per_seq=seq_len // tm, conv_width=conv_width),
        grid=(n_blocks + 1,),
        in_specs=[
            pl.BlockSpec((tm, d), lambda s: (jnp.minimum(s, n_blocks - 1), 0)),
            pl.BlockSpec((tm, d), lagged),
            pl.BlockSpec((None, 1, d), lambda s: (layer, 0, 0)),
            _resident((d, 2 * r), lambda s: (0, 0)),
            pl.BlockSpec((None, conv_width, r), lambda s: (j, 0, 0)),
            vec(j),
            _resident((None, heads, hd_dim, 2 * hd_dim), lambda s: (j, 0, 0, 0)),
            vec(j),
            vec(j),
            vec(j),
            _resident((r, d), lambda s: (0, 0)),
        ],
        out_specs=pl.BlockSpec((tm, d), lagged),
        out_shape=jax.ShapeDtypeStruct((m, d), F32),
        scratch_shapes=[
            proj,
            proj,
            pltpu.VMEM((tm, r), F32),
            pltpu.VMEM((tm, r), F32),
            pltpu.VMEM((tm, r), BF16),
            pltpu.VMEM((SUBLANES_F32, r), F32),
        ],
        compiler_params=_params(("arbitrary",)),
        name=f"lru_l{layer}",
    )(h, h, norms, w_in, conv_w, conv_b, w_ax, b_a, b_x, lam, w_out)


def _kv_kernel(m_ref, g_ref, wk_ref, wv_ref, fg_ref, fu_ref, fd_ref, wq_ref, wo_ref,
               k_out, v_out, fg_out, fu_out, fd_out, wq_out, wo_out, mn_ref):
    first = jnp.logical_and(pl.program_id(0) == 0, pl.program_id(1) == 0)

    @pl.when(first)
    def _():
        mn_ref[...] = _rmsnorm(m_ref[...], g_ref[...]).astype(BF16)

    mn = mn_ref[...]
    k_out[...] = jnp.dot(mn, wk_ref[...].astype(BF16), preferred_element_type=F32).astype(BF16)
    v_out[...] = jnp.dot(mn, wv_ref[...].astype(BF16), preferred_element_type=F32).astype(BF16)

    tf = fg_out.shape[2]
    for j in range(fg_out.shape[0]):
        fg_out[j] = fg_ref[:, j * tf:(j + 1) * tf].astype(BF16)
        fu_out[j] = fu_ref[:, j * tf:(j + 1) * tf].astype(BF16)
    fd_out[...] = fd_ref[...].astype(BF16)
    wq_out[...] = wq_ref[...].astype(BF16)
    wo_out[...] = wo_ref[...].astype(BF16)


def _kv(mem2d, mem_norm, w_k, w_v, w_gate, w_up, w_down, w_q, w_o):
    rows, d = mem2d.shape
    depth = w_k.shape[0]
    f = w_down.shape[2]
    tn, tf = KV_COLUMN_BLOCK, FFN_HIDDEN_BLOCK
    n_n = d // tn
    steps = depth * n_n
    step = lambda l, n: l * n_n + n
    kv_out = jax.ShapeDtypeStruct((depth, rows, d), BF16)
    outs = pl.pallas_call(
        _kv_kernel,
        grid=(depth, n_n),
        in_specs=[
            _resident((rows, d), lambda l, n: (0, 0)),
            pl.BlockSpec((1, d), lambda l, n: (0, 0)),
            pl.BlockSpec((None, d, tn), lambda l, n: (l, 0, n)),
            pl.BlockSpec((None, d, tn), lambda l, n: (l, 0, n)),
            pl.BlockSpec((None, None, d // steps, f), lambda l, n: (0, 0, step(l, n), 0)),
            pl.BlockSpec((None, None, d // steps, f), lambda l, n: (0, 0, step(l, n), 0)),
            pl.BlockSpec((None, None, f // steps, d), lambda l, n: (0, 0, step(l, n), 0)),
            pl.BlockSpec((None, d // steps, d), lambda l, n: (0, step(l, n), 0)),
            pl.BlockSpec((None, d // steps, d), lambda l, n: (0, step(l, n), 0)),
        ],
        out_specs=[
            pl.BlockSpec((None, rows, tn), lambda l, n: (l, 0, n)),
            pl.BlockSpec((None, rows, tn), lambda l, n: (l, 0, n)),
            pl.BlockSpec((f // tf, d // steps, tf), lambda l, n: (0, step(l, n), 0)),
            pl.BlockSpec((f // tf, d // steps, tf), lambda l, n: (0, step(l, n), 0)),
            pl.BlockSpec((f // steps, d), lambda l, n: (step(l, n), 0)),
            pl.BlockSpec((d // steps, d), lambda l, n: (step(l, n), 0)),
            pl.BlockSpec((d // steps, d), lambda l, n: (step(l, n), 0)),
        ],
        out_shape=[kv_out, kv_out,
                   jax.ShapeDtypeStruct((f // tf, d, tf), BF16), jax.ShapeDtypeStruct((f // tf, d, tf), BF16),
                   jax.ShapeDtypeStruct((f, d), BF16),
                   jax.ShapeDtypeStruct((d, d), BF16), jax.ShapeDtypeStruct((d, d), BF16)],
        scratch_shapes=[pltpu.VMEM((rows, d), BF16)],
        compiler_params=_params(("arbitrary", "arbitrary")),
        name="kv_proj",
    )(mem2d, mem_norm, w_k, w_v, w_gate, w_up, w_down, w_q, w_o)
    return outs[0], outs[1], tuple(outs[2:5]), tuple(outs[5:7])


def _xattn_kernel(*refs, n_casts):
    h_ref, g_ref, wq_ref, k_ref, v_ref, wo_ref = refs[:6]
    cast_in = refs[6:6 + n_casts]
    o_ref = refs[6 + n_casts]
    cast_out = refs[7 + n_casts:7 + 2 * n_casts]
    q_ref, a_ref = refs[7 + 2 * n_casts:]
    d = h_ref.shape[1]
    hd_dim = d // XATTN_HEADS
    scale = hd_dim ** -0.5
    x = h_ref[...]
    u = _rmsnorm(x, g_ref[...]).astype(BF16)
    q_ref[...] = jnp.dot(u, wq_ref[...], preferred_element_type=F32).astype(BF16)
    head_cols = [slice(hd * hd_dim, (hd + 1) * hd_dim) for hd in range(XATTN_HEADS)]
    scores = [lax.dot_general(q_ref[:, cols], k_ref[:, cols], (((1,), (1,)), ((), ())),
                              preferred_element_type=F32) * scale for cols in head_cols]
    probs = []
    for s in scores:
        e = jnp.exp(s - jnp.max(s, axis=-1, keepdims=True))
        probs.append((e * (1.0 / jnp.sum(e, axis=-1, keepdims=True))).astype(BF16))
    for cols, p in zip(head_cols, probs):
        a_ref[:, cols] = jnp.dot(p, v_ref[:, cols], preferred_element_type=F32).astype(BF16)
    o_ref[...] = x + jnp.dot(a_ref[...], wo_ref[...], preferred_element_type=F32)
    for src, dst in zip(cast_in, cast_out):
        dst[...] = src[...].astype(BF16)


def _xattn(h, norm, w_q, k_all, v_all, w_o, layer, seq_len, mem_len, casts):
    m, d = h.shape
    tm = XATTN_TOKEN_BLOCK
    n_blocks = m // tm
    blocks_per_seq = seq_len // tm
    in_specs = [
        pl.BlockSpec((tm, d), lambda i: (i, 0)),
        pl.BlockSpec((None, 1, d), lambda i: (layer, 0, 0)),
        _resident((d, d), lambda i: (0, 0)),
        pl.BlockSpec((None, mem_len, d), lambda i: (layer, i // blocks_per_seq, 0)),
        pl.BlockSpec((None, mem_len, d), lambda i: (layer, i // blocks_per_seq, 0)),
        _resident((d, d), lambda i: (0, 0)),
    ]
    out_specs = [pl.BlockSpec((tm, d), lambda i: (i, 0))]
    out_shape = [jax.ShapeDtypeStruct((m, d), F32)]
    args = [h, norm, w_q, k_all, v_all, w_o]
    for stacked, idx in casts:
        _, rows, cols = stacked.shape
        slab = rows // n_blocks
        in_specs.append(pl.BlockSpec((None, slab, cols), lambda i, idx=idx: (idx, i, 0)))
        out_specs.append(pl.BlockSpec((slab, cols), lambda i: (i, 0)))
        out_shape.append(jax.ShapeDtypeStruct((rows, cols), BF16))
        args.append(stacked)
    outs = pl.pallas_call(
        functools.partial(_xattn_kernel, n_casts=len(casts)),
        grid=(n_blocks,),
        in_specs=in_specs,
        out_specs=out_specs,
        out_shape=out_shape,
        scratch_shapes=[pltpu.VMEM((tm, d), BF16), pltpu.VMEM((tm, d), BF16)],
        compiler_params=_params(("arbitrary",)),
        name=f"xattn_l{layer}",
    )(*args)
    return outs[0], list(outs[1:])


def kernel(x, mem, ffn_norm, w_ffn_gate, w_ffn_up, w_ffn_down, mix_norm, pool_w, pool_scale, lru_w_in, lru_conv_w, lru_conv_b, lru_w_a, lru_b_a, lru_w_x, lru_b_x, lru_lambda, lru_w_out, xattn_norm, mem_norm, w_q, w_k, w_v, w_o, final_norm):
    batch, seq_len, d = x.shape
    mem_len = mem.shape[1]
    depth = ffn_norm.shape[0]
    n_mixers = 2

    row = lambda a: a.reshape(-1, 1, a.shape[-1])
    ffn_norm_r, mix_norm_r, xattn_norm_r = row(ffn_norm), row(mix_norm), row(xattn_norm)
    pool_scale_r = row(pool_scale)
    conv_b_r, b_a_r, b_x_r, lam_r = row(lru_conv_b), row(lru_b_a), row(lru_b_x), row(lru_lambda)
    final_g = final_norm.reshape(1, d)

    pool_w_b = pool_w.astype(BF16)
    w_ax = jnp.concatenate([lru_w_a, lru_w_x], axis=-1).astype(BF16)
    lru_w = None

    k_all, v_all, ffn_w, attn_w = _kv(mem.reshape(batch * mem_len, d), mem_norm.reshape(1, d), w_k, w_v,
                                      w_ffn_gate, w_ffn_up, w_ffn_down, w_q, w_o)

    def ffn(h, ffn_w, layer, half):
        last = layer == depth - 1 and half == 1
        nxt = None if last else (w_ffn_gate, w_ffn_up, w_ffn_down) + ((layer, 1) if half == 0 else (layer + 1, 0))
        outs = _ffn(h, ffn_norm_r, layer * 2 + half, *ffn_w, final_g, last, nxt, f"ffn_l{layer}_{half}")
        return (outs, None) if last else (outs[0], tuple(outs[1:]))

    h = x.reshape(batch * seq_len, d)
    for i in range(depth):
        j = i // n_mixers
        h, ffn_w = ffn(h, ffn_w, i, 0)
        if i % n_mixers == 0:
            h = _pool(h, mix_norm_r, pool_w_b, pool_scale_r, i, j, seq_len)
        else:
            h = _lru(h, mix_norm_r, lru_w[0], lru_conv_w, conv_b_r, w_ax, b_a_r, b_x_r, lam_r, lru_w[1],
                     i, j, seq_len)
        casts = []
        if i + 1 < depth:
            casts += [(w_q, i + 1), (w_o, i + 1)]
            if (i + 1) % n_mixers == 1:
                casts += [(lru_w_in, (i + 1) // n_mixers), (lru_w_out, (i + 1) // n_mixers)]
        h, cast = _xattn(h, xattn_norm_r, attn_w[0], k_all, v_all, attn_w[1], i, seq_len, mem_len, casts)
        if i + 1 < depth:
            attn_w = tuple(cast[:2])
            if (i + 1) % n_mixers == 1:
                lru_w = tuple(cast[2:])
        h, ffn_w = ffn(h, ffn_w, i, 1)
    return h.reshape(batch, seq_len, d)
```

```python
import functools

import jax
import jax.numpy as jnp
from jax import lax
from jax.experimental import pallas as pl
from jax.experimental.pallas import tpu as pltpu

F32 = jnp.float32
BF16 = jnp.bfloat16

EPS = 1e-6
MACARON_WEIGHT = 0.5
POOL_WINDOWS = (2, 4, 8, 16)
LRU_HEADS = 16
LRU_C = 8.0
XATTN_HEADS = 4

SUBLANES_F32 = 8
SUBLANES_BF16 = 16
VMEM_LIMIT_BYTES = 58 * 1024 * 1024

FFN_TOKEN_BLOCK = 1024
FFN_HIDDEN_BLOCK = 512
POOL_TOKEN_BLOCK = 512
LRU_TOKEN_BLOCK = 256
LRU_PROJ_CHUNKS = 16
XATTN_TOKEN_BLOCK = 512
WEIGHT_TILE = 512
KV_COLUMN_BLOCK = 256


def _rmsnorm(x, g):
    ms = jnp.mean(x * x, axis=-1, keepdims=True)
    return x * lax.rsqrt(ms + EPS) * g


def _params(semantics):
    return pltpu.CompilerParams(dimension_semantics=semantics, vmem_limit_bytes=VMEM_LIMIT_BYTES)


def _cast_to_tiles(src, dst):
    tile = dst.shape[2]
    for t in range(dst.shape[0]):
        dst[t] = src[:, t * tile:(t + 1) * tile].astype(BF16)


def _resident(block_shape, index_map):
    return pl.BlockSpec(block_shape, index_map, pipeline_mode=pl.Buffered(1))


def _ffn_kernel(*refs, apply_final_norm, cast_next):
    h_ref, g_ref, wg_ref, wu_ref, wd_ref, fg_ref = refs[:6]
    if cast_next:
        ng_ref, nu_ref, nd_ref, o_ref, ng_out, nu_out, nd_out, u_ref = refs[6:]
    else:
        o_ref, u_ref = refs[6:]
    j = pl.program_id(1)

    @pl.when(j == 0)
    def _():
        x = h_ref[...]
        u_ref[...] = _rmsnorm(x, g_ref[...]).astype(BF16)
        o_ref[...] = x

    u = u_ref[...]
    gate = jnp.dot(u, wg_ref[...], preferred_element_type=F32)
    up = jnp.dot(u, wu_ref[...], preferred_element_type=F32)
    act = (jax.nn.silu(gate) * up * MACARON_WEIGHT).astype(BF16)
    o_ref[...] += jnp.dot(act, wd_ref[...], preferred_element_type=F32)

    if cast_next:
        ng_out[...] = ng_ref[...].astype(BF16)
        nu_out[...] = nu_ref[...].astype(BF16)
        nd_out[...] = nd_ref[...].astype(BF16)

    if apply_final_norm:

        @pl.when(j == pl.num_programs(1) - 1)
        def _():
            o_ref[...] = _rmsnorm(o_ref[...], fg_ref[...])


def _ffn(h, norms, norm_row, wg, wu, wd, final_g, apply_final_norm, nxt, name):
    m, d = h.shape
    tm, tf = FFN_TOKEN_BLOCK, FFN_HIDDEN_BLOCK
    f = wd.shape[0]
    n_i = m // tm
    in_specs = [
        pl.BlockSpec((tm, d), lambda i, j: (i, 0)),
        pl.BlockSpec((None, 1, d), lambda i, j: (norm_row, 0, 0)),
        pl.BlockSpec((None, d, tf), lambda i, j: (j, 0, 0)),
        pl.BlockSpec((None, d, tf), lambda i, j: (j, 0, 0)),
        pl.BlockSpec((tf, d), lambda i, j: (j, 0)),
        pl.BlockSpec((1, d), lambda i, j: (0, 0)),
    ]
    out_specs = [pl.BlockSpec((tm, d), lambda i, j: (i, 0))]
    out_shape = [jax.ShapeDtypeStruct((m, d), F32)]
    args = [h, norms, wg, wu, wd, final_g]
    if nxt is not None:
        n_gate, n_up, n_down, n_layer, n_half = nxt
        ds = d // n_i
        in_specs += [
            pl.BlockSpec((None, None, ds, tf), lambda i, j: (n_layer, n_half, i, j)),
            pl.BlockSpec((None, None, ds, tf), lambda i, j: (n_layer, n_half, i, j)),
            pl.BlockSpec((None, None, tf, ds), lambda i, j: (n_layer, n_half, j, i)),
        ]
        out_specs += [
            pl.BlockSpec((None, ds, tf), lambda i, j: (j, i, 0)),
            pl.BlockSpec((None, ds, tf), lambda i, j: (j, i, 0)),
            pl.BlockSpec((tf, ds), lambda i, j: (j, i)),
        ]
        tiled = jax.ShapeDtypeStruct((f // tf, d, tf), BF16)
        out_shape += [tiled, tiled, jax.ShapeDtypeStruct((f, d), BF16)]
        args += [n_gate, n_up, n_down]
    outs = pl.pallas_call(
        functools.partial(_ffn_kernel, apply_final_norm=apply_final_norm, cast_next=nxt is not None),
        grid=(n_i, f // tf),
        in_specs=in_specs,
        out_specs=out_specs,
        out_shape=out_shape,
        scratch_shapes=[pltpu.VMEM((tm, d), BF16)],
        compiler_params=_params(("parallel", "arbitrary")),
        name=name,
    )(*args)
    return outs if nxt is not None else outs[0]


def _pool_kernel(h_ref, g_ref, w_ref, sc_ref, o_ref, buf_ref, tmp0_ref, tmp1_ref, *, blocks_per_seq, halo):
    tm = h_ref.shape[0]
    groups = len(POOL_WINDOWS)
    gd = h_ref.shape[1] // groups
    pad = 2 * halo
    rows = pad + tm
    blk = pl.program_id(0) % blocks_per_seq

    @pl.when(pl.program_id(0) == 0)
    def _():
        buf_ref[0:halo, :] = jnp.zeros((halo, buf_ref.shape[1]), F32)
        tmp0_ref[...] = jnp.zeros(tmp0_ref.shape, F32)
        tmp1_ref[...] = jnp.zeros(tmp1_ref.shape, F32)

    @pl.when(blk == 0)
    def _():
        buf_ref[halo:pad, :] = jnp.zeros((halo, buf_ref.shape[1]), F32)

    x = h_ref[...]
    buf_ref[pad:, :] = _rmsnorm(x, g_ref[...])
    pos = (blk * tm + 1 + lax.broadcasted_iota(jnp.int32, (tm, 1), 0)).astype(F32)

    for g, w in enumerate(POOL_WINDOWS):
        cols = slice(g * gd, (g + 1) * gd)
        src, src_cols = buf_ref, cols
        shift = 1
        for dst in (tmp0_ref, tmp1_ref, tmp0_ref, tmp1_ref):
            if shift >= w:
                break
            dst[halo:, :] = src[halo:rows, src_cols] + src[halo - shift:rows - shift, src_cols]
            src, src_cols = dst, slice(None)
            shift *= 2
        inv_count = 1.0 / jnp.minimum(pos, float(w))
        pooled = (src[pad:, src_cols] * inv_count - buf_ref[pad:, cols]).astype(BF16)
        y = jnp.dot(pooled, w_ref[g], preferred_element_type=F32)
        o_ref[:, cols] = x[:, cols] + y * sc_ref[:, cols]

    buf_ref[halo:pad, :] = buf_ref[tm + halo:tm + pad, :]


def _pool(h, norms, pool_w, pool_scale, layer, j, seq_len):
    m, d = h.shape
    tm = POOL_TOKEN_BLOCK
    halo = max(POOL_WINDOWS)
    groups, gd = pool_w.shape[1], pool_w.shape[2]
    return pl.pallas_call(
        functools.partial(_pool_kernel, blocks_per_seq=seq_len // tm, halo=halo),
        grid=(m // tm,),
        in_specs=[
            pl.BlockSpec((tm, d), lambda i: (i, 0)),
            pl.BlockSpec((None, 1, d), lambda i: (layer, 0, 0)),
            _resident((None, groups, gd, gd), lambda i: (j, 0, 0, 0)),
            pl.BlockSpec((None, 1, d), lambda i: (j, 0, 0)),
        ],
        out_specs=pl.BlockSpec((tm, d), lambda i: (i, 0)),
        out_shape=jax.ShapeDtypeStruct((m, d), F32),
        scratch_shapes=[
            pltpu.VMEM((2 * halo + tm, d), F32),
            pltpu.VMEM((2 * halo + tm, gd), F32),
            pltpu.VMEM((2 * halo + tm, gd), F32),
        ],
        compiler_params=_params(("arbitrary",)),
        name=f"pool_l{layer}",
    )(h, norms, pool_w, pool_scale)


def _softplus(z):
    return jnp.maximum(z, 0.0) + jnp.log1p(jnp.exp(-jnp.abs(z)))


def _sigmoid(z):
    return 0.5 * jnp.tanh(0.5 * z) + 0.5


def _lru_kernel(h_ref, hlag_ref, g_ref, win_ref, cw_ref, cb_ref, wax_ref, ba_ref, bx_ref, lam_ref, wout_ref,
                o_ref, proj0_ref, proj1_ref, a_ref, b_ref, y_ref, carry_ref, *, blocks_per_seq, conv_width):
    tm = h_ref.shape[0]
    r = a_ref.shape[1]
    hd_dim = r // LRU_HEADS
    tail = SUBLANES_F32
    s = pl.program_id(0)
    drain_starts_seq = (s + blocks_per_seq - 1) % blocks_per_seq == 0
    fill_starts_seq = s % blocks_per_seq == 0

    @pl.when(s == 0)
    def _():
        proj0_ref[...] = jnp.zeros(proj0_ref.shape, F32)
        proj1_ref[...] = jnp.zeros(proj1_ref.shape, F32)
        carry_ref[...] = jnp.zeros(carry_ref.shape, F32)

    def step(fill_ref, drain_ref):
        u = _rmsnorm(h_ref[...], g_ref[...]).astype(BF16)
        decay = _softplus(-lam_ref[...])
        rows = lax.broadcasted_iota(jnp.int32, (SUBLANES_F32, r), 0)
        chunk = 2 * r // LRU_PROJ_CHUNKS

        def project(c):
            cols = slice(c * chunk, (c + 1) * chunk)
            fill_ref[tail:, cols] = jnp.dot(u, win_ref[c], preferred_element_type=F32)

        def gates(hd):
            cols = slice(hd * hd_dim, (hd + 1) * hd_dim)
            xcols = slice(r + hd * hd_dim, r + (hd + 1) * hd_dim)
            xc = cb_ref[:, cols]
            for k in range(conv_width):
                back = conv_width - 1 - k
                xc = xc + drain_ref[tail - back:tail - back + tm, xcols] * cw_ref[k:k + 1, cols]
            ra = jnp.dot(xc.astype(BF16), wax_ref[hd], preferred_element_type=F32)
            rg = _sigmoid(ra[:, :hd_dim] + ba_ref[:, cols])
            ig = _sigmoid(ra[:, hd_dim:] + bx_ref[:, cols])
            a = jnp.exp((-LRU_C) * rg * decay[:, cols])
            v = 1.0 - a * a
            a_ref[:, cols] = a
            b_ref[:, cols] = jnp.where(v > 0.0, v * lax.rsqrt(v), 0.0) * ig * xc

        def scan_group(row0, carry):
            a = a_ref[row0:row0 + SUBLANES_F32, :]
            b = b_ref[row0:row0 + SUBLANES_F32, :]
            for dist in (1, 2, 4):
                keep = rows >= dist
                a_prev = jnp.where(keep, pltpu.roll(a, dist, 0), 1.0)
                b_prev = jnp.where(keep, pltpu.roll(b, dist, 0), 0.0)
                b = a * b_prev + b
                a = a * a_prev
            hh = b + a * carry
            gate = drain_ref[tail + row0:tail + row0 + SUBLANES_F32, 0:r]
            y = hh * jax.nn.gelu(gate)
            return y, jnp.broadcast_to(hh[SUBLANES_F32 - 1:, :], (SUBLANES_F32, r))

        state = {"carry": jnp.where(drain_starts_seq, 0.0, carry_ref[...])}

        def scan_pair(row0):
            y0, carry = scan_group(row0, state["carry"])
            y1, carry = scan_group(row0 + SUBLANES_F32, carry)
            y_ref[row0:row0 + SUBLANES_BF16, :] = jnp.concatenate([y0, y1], axis=0).astype(BF16)
            state["carry"] = carry

        vector_tasks = [functools.partial(gates, hd) for hd in range(LRU_HEADS)]
        vector_tasks += [functools.partial(scan_pair, row0) for row0 in range(0, tm, SUBLANES_BF16)]
        per_chunk = -(-len(vector_tasks) // LRU_PROJ_CHUNKS)
        for c in range(LRU_PROJ_CHUNKS):
            project(c)
            for task in vector_tasks[c * per_chunk:(c + 1) * per_chunk]:
                task()

        fill_ref[0:tail, r:] = jnp.where(fill_starts_seq, 0.0, drain_ref[tm:tm + tail, r:])
        carry_ref[...] = state["carry"]
        out_tile = wout_ref.shape[2]
        for t in range(wout_ref.shape[0]):
            cols = slice(t * out_tile, (t + 1) * out_tile)
            o_ref[:, cols] = hlag_ref[:, cols] + jnp.dot(y_ref[...], wout_ref[t], preferred_element_type=F32)

    @pl.when(s % 2 == 0)
    def _():
        step(proj0_ref, proj1_ref)

    @pl.when(s % 2 == 1)
    def _():
        step(proj1_ref, proj0_ref)


def _lru(h, norms, w_in, conv_w, conv_b, w_ax, b_a, b_x, lam, w_out, layer, j, seq_len):
    m, d = h.shape
    tm = LRU_TOKEN_BLOCK
    n_blocks = m // tm
    r = w_out.shape[1]
    conv_width = conv_w.shape[1]
    heads, hd_dim = w_ax.shape[1], w_ax.shape[2]
    vec = lambda idx: pl.BlockSpec((None, 1, r), lambda s: (idx, 0, 0))
    lagged = lambda s: (jnp.maximum(s - 1, 0), 0)
    proj = pltpu.VMEM((SUBLANES_F32 + tm, 2 * r), F32)
    return pl.pallas_call(
        functools.partial(_lru_kernel, blocks_per_seq=seq_len // tm, conv_width=conv_width),
        grid=(n_blocks + 1,),
        in_specs=[
            pl.BlockSpec((tm, d), lambda s: (jnp.minimum(s, n_blocks - 1), 0)),
            pl.BlockSpec((tm, d), lagged),
            pl.BlockSpec((None, 1, d), lambda s: (layer, 0, 0)),
            _resident((LRU_PROJ_CHUNKS, d, 2 * r // LRU_PROJ_CHUNKS), lambda s: (0, 0, 0)),
            pl.BlockSpec((None, conv_width, r), lambda s: (j, 0, 0)),
            vec(j),
            _resident((None, heads, hd_dim, 2 * hd_dim), lambda s: (j, 0, 0, 0)),
            vec(j),
            vec(j),
            vec(j),
            _resident(w_out.shape, lambda s: (0, 0, 0)),
        ],
        out_specs=pl.BlockSpec((tm, d), lagged),
        out_shape=jax.ShapeDtypeStruct((m, d), F32),
        scratch_shapes=[
            proj,
            proj,
            pltpu.VMEM((tm, r), F32),
            pltpu.VMEM((tm, r), F32),
            pltpu.VMEM((tm, r), BF16),
            pltpu.VMEM((SUBLANES_F32, r), F32),
        ],
        compiler_params=_params(("arbitrary",)),
        name=f"lru_l{layer}",
    )(h, h, norms, w_in, conv_w, conv_b, w_ax, b_a, b_x, lam, w_out)


def _kv_kernel(m_ref, g_ref, wk_ref, wv_ref, fg_ref, fu_ref, fd_ref, wq_ref, wo_ref,
               k_out, v_out, fg_out, fu_out, fd_out, wq_out, wo_out, mn_ref):
    first = jnp.logical_and(pl.program_id(0) == 0, pl.program_id(1) == 0)

    @pl.when(first)
    def _():
        mn_ref[...] = _rmsnorm(m_ref[...], g_ref[...]).astype(BF16)

    mn = mn_ref[...]
    k_out[...] = jnp.dot(mn, wk_ref[...].astype(BF16), preferred_element_type=F32).astype(BF16)
    v_out[...] = jnp.dot(mn, wv_ref[...].astype(BF16), preferred_element_type=F32).astype(BF16)

    _cast_to_tiles(fg_ref, fg_out)
    _cast_to_tiles(fu_ref, fu_out)
    fd_out[...] = fd_ref[...].astype(BF16)
    _cast_to_tiles(wq_ref, wq_out)
    _cast_to_tiles(wo_ref, wo_out)


def _kv(mem2d, mem_norm, w_k, w_v, w_gate, w_up, w_down, w_q, w_o):
    rows, d = mem2d.shape
    depth = w_k.shape[0]
    f = w_down.shape[2]
    tn, tf = KV_COLUMN_BLOCK, FFN_HIDDEN_BLOCK
    n_n = d // tn
    steps = depth * n_n
    step = lambda l, n: l * n_n + n
    kv_out = jax.ShapeDtypeStruct((depth, rows, d), BF16)
    outs = pl.pallas_call(
        _kv_kernel,
        grid=(depth, n_n),
        in_specs=[
            _resident((rows, d), lambda l, n: (0, 0)),
            pl.BlockSpec((1, d), lambda l, n: (0, 0)),
            pl.BlockSpec((None, d, tn), lambda l, n: (l, 0, n)),
            pl.BlockSpec((None, d, tn), lambda l, n: (l, 0, n)),
            pl.BlockSpec((None, None, d // steps, f), lambda l, n: (0, 0, step(l, n), 0)),
            pl.BlockSpec((None, None, d // steps, f), lambda l, n: (0, 0, step(l, n), 0)),
            pl.BlockSpec((None, None, f // steps, d), lambda l, n: (0, 0, step(l, n), 0)),
            pl.BlockSpec((None, d // steps, d), lambda l, n: (0, step(l, n), 0)),
            pl.BlockSpec((None, d // steps, d), lambda l, n: (0, step(l, n), 0)),
        ],
        out_specs=[
            pl.BlockSpec((None, rows, tn), lambda l, n: (l, 0, n)),
            pl.BlockSpec((None, rows, tn), lambda l, n: (l, 0, n)),
            pl.BlockSpec((f // tf, d // steps, tf), lambda l, n: (0, step(l, n), 0)),
            pl.BlockSpec((f // tf, d // steps, tf), lambda l, n: (0, step(l, n), 0)),
            pl.BlockSpec((f // steps, d), lambda l, n: (step(l, n), 0)),
            pl.BlockSpec((d // WEIGHT_TILE, d // steps, WEIGHT_TILE), lambda l, n: (0, step(l, n), 0)),
            pl.BlockSpec((d // WEIGHT_TILE, d // steps, WEIGHT_TILE), lambda l, n: (0, step(l, n), 0)),
        ],
        out_shape=[kv_out, kv_out,
                   jax.ShapeDtypeStruct((f // tf, d, tf), BF16), jax.ShapeDtypeStruct((f // tf, d, tf), BF16),
                   jax.ShapeDtypeStruct((f, d), BF16),
                   jax.ShapeDtypeStruct((d // WEIGHT_TILE, d, WEIGHT_TILE), BF16),
                   jax.ShapeDtypeStruct((d // WEIGHT_TILE, d, WEIGHT_TILE), BF16)],
        scratch_shapes=[pltpu.VMEM((rows, d), BF16)],
        compiler_params=_params(("arbitrary", "arbitrary")),
        name="kv_proj",
    )(mem2d, mem_norm, w_k, w_v, w_gate, w_up, w_down, w_q, w_o)
    return outs[0], outs[1], tuple(outs[2:5]), tuple(outs[5:7])


def _xattn_kernel(*refs, n_casts):
    h_ref, g_ref, wq_ref, k_ref, v_ref, wo_ref = refs[:6]
    cast_in = refs[6:6 + n_casts]
    o_ref = refs[6 + n_casts]
    cast_out = refs[7 + n_casts:7 + 2 * n_casts]
    q_ref, a_ref = refs[7 + 2 * n_casts:]
    d = h_ref.shape[1]
    hd_dim = d // XATTN_HEADS
    scale = hd_dim ** -0.5
    x = h_ref[...]
    u = _rmsnorm(x, g_ref[...]).astype(BF16)
    tile = wq_ref.shape[2]
    for t in range(wq_ref.shape[0]):
        q_ref[:, t * tile:(t + 1) * tile] = jnp.dot(u, wq_ref[t], preferred_element_type=F32).astype(BF16)
    head_cols = [slice(hd * hd_dim, (hd + 1) * hd_dim) for hd in range(XATTN_HEADS)]
    scores = [lax.dot_general(q_ref[:, cols], k_ref[:, cols], (((1,), (1,)), ((), ())),
                              preferred_element_type=F32) * scale for cols in head_cols]
    probs = []
    for s in scores:
        e = jnp.exp(s - jnp.max(s, axis=-1, keepdims=True))
        probs.append((e * (1.0 / jnp.sum(e, axis=-1, keepdims=True))).astype(BF16))
    for cols, p in zip(head_cols, probs):
        a_ref[:, cols] = jnp.dot(p, v_ref[:, cols], preferred_element_type=F32).astype(BF16)
    for t in range(wo_ref.shape[0]):
        cols = slice(t * tile, (t + 1) * tile)
        o_ref[:, cols] = x[:, cols] + jnp.dot(a_ref[...], wo_ref[t], preferred_element_type=F32)
    for src, dst in zip(cast_in, cast_out):
        _cast_to_tiles(src, dst)


def _xattn(h, norm, w_q, k_all, v_all, w_o, layer, seq_len, mem_len, casts):
    m, d = h.shape
    tm = XATTN_TOKEN_BLOCK
    n_blocks = m // tm
    blocks_per_seq = seq_len // tm
    in_specs = [
        pl.BlockSpec((tm, d), lambda i: (i, 0)),
        pl.BlockSpec((None, 1, d), lambda i: (layer, 0, 0)),
        _resident(w_q.shape, lambda i: (0, 0, 0)),
        pl.BlockSpec((None, mem_len, d), lambda i: (layer, i // blocks_per_seq, 0)),
        pl.BlockSpec((None, mem_len, d), lambda i: (layer, i // blocks_per_seq, 0)),
        _resident(w_o.shape, lambda i: (0, 0, 0)),
    ]
    out_specs = [pl.BlockSpec((tm, d), lambda i: (i, 0))]
    out_shape = [jax.ShapeDtypeStruct((m, d), F32)]
    args = [h, norm, w_q, k_all, v_all, w_o]
    for stacked, idx, tile in casts:
        _, rows, cols = stacked.shape
        slab = rows // n_blocks
        in_specs.append(pl.BlockSpec((None, slab, cols), lambda i, idx=idx: (idx, i, 0)))
        out_specs.append(pl.BlockSpec((cols // tile, slab, tile), lambda i: (0, i, 0)))
        out_shape.append(jax.ShapeDtypeStruct((cols // tile, rows, tile), BF16))
        args.append(stacked)
    outs = pl.pallas_call(
        functools.partial(_xattn_kernel, n_casts=len(casts)),
        grid=(n_blocks,),
        in_specs=in_specs,
        out_specs=out_specs,
        out_shape=out_shape,
        scratch_shapes=[pltpu.VMEM((tm, d), BF16), pltpu.VMEM((tm, d), BF16)],
        compiler_params=_params(("arbitrary",)),
        name=f"xattn_l{layer}",
    )(*args)
    return outs[0], list(outs[1:])


def kernel(x, mem, ffn_norm, w_ffn_gate, w_ffn_up, w_ffn_down, mix_norm, pool_w, pool_scale, lru_w_in, lru_conv_w, lru_conv_b, lru_w_a, lru_b_a, lru_w_x, lru_b_x, lru_lambda, lru_w_out, xattn_norm, mem_norm, w_q, w_k, w_v, w_o, final_norm):
    batch, seq_len, d = x.shape
    mem_len = mem.shape[1]
    depth = ffn_norm.shape[0]
    n_mixers = 2

    row = lambda a: a.reshape(-1, 1, a.shape[-1])
    ffn_norm_r, mix_norm_r, xattn_norm_r = row(ffn_norm), row(mix_norm), row(xattn_norm)
    pool_scale_r = row(pool_scale)
    conv_b_r, b_a_r, b_x_r, lam_r = row(lru_conv_b), row(lru_b_a), row(lru_b_x), row(lru_lambda)
    final_g = final_norm.reshape(1, d)

    pool_w_b = pool_w.astype(BF16)
    w_ax = jnp.concatenate([lru_w_a, lru_w_x], axis=-1).astype(BF16)
    lru_w = None

    k_all, v_all, ffn_w, attn_w = _kv(mem.reshape(batch * mem_len, d), mem_norm.reshape(1, d), w_k, w_v,
                                      w_ffn_gate, w_ffn_up, w_ffn_down, w_q, w_o)

    def ffn(h, ffn_w, layer, half):
        last = layer == depth - 1 and half == 1
        nxt = None if last else (w_ffn_gate, w_ffn_up, w_ffn_down) + ((layer, 1) if half == 0 else (layer + 1, 0))
        outs = _ffn(h, ffn_norm_r, layer * 2 + half, *ffn_w, final_g, last, nxt, f"ffn_l{layer}_{half}")
        return (outs, None) if last else (outs[0], tuple(outs[1:]))

    h = x.reshape(batch * seq_len, d)
    for i in range(depth):
        j = i // n_mixers
        h, ffn_w = ffn(h, ffn_w, i, 0)
        if i % n_mixers == 0:
            h = _pool(h, mix_norm_r, pool_w_b, pool_scale_r, i, j, seq_len)
        else:
            h = _lru(h, mix_norm_r, lru_w[0], lru_conv_w, conv_b_r, w_ax, b_a_r, b_x_r, lam_r, lru_w[1],
                     i, j, seq_len)
        casts = []
        if i + 1 < depth:
            casts += [(w_q, i + 1, WEIGHT_TILE), (w_o, i + 1, WEIGHT_TILE)]
            if (i + 1) % n_mixers == 1:
                proj_tile = lru_w_in.shape[2] // LRU_PROJ_CHUNKS
                casts += [(lru_w_in, (i + 1) // n_mixers, proj_tile), (lru_w_out, (i + 1) // n_mixers, WEIGHT_TILE)]
        h, cast = _xattn(h, xattn_norm_r, attn_w[0], k_all, v_all, attn_w[1], i, seq_len, mem_len, casts)
        if i + 1 < depth:
            attn_w = tuple(cast[:2])
            if (i + 1) % n_mixers == 1:
                lru_w = tuple(cast[2:])
        h, ffn_w = ffn(h, ffn_w, i, 1)
    return h.reshape(batch, seq_len, d)
```

```python
import functools

import jax
import jax.numpy as jnp
from jax import lax
from jax.experimental import pallas as pl
from jax.experimental.pallas import tpu as pltpu

F32 = jnp.float32
BF16 = jnp.bfloat16

EPS = 1e-6
MACARON_WEIGHT = 0.5
POOL_WINDOWS = (2, 4, 8, 16)
LRU_HEADS = 16
LRU_C = 8.0
XATTN_HEADS = 4

SUBLANES_F32 = 8
SUBLANES_BF16 = 16
VMEM_LIMIT_BYTES = 58 * 1024 * 1024

FFN_TOKEN_BLOCK = 1024
FFN_HIDDEN_BLOCK = 512
POOL_TOKEN_BLOCK = 512
LRU_TOKEN_BLOCK = 256
LRU_PROJ_CHUNKS = 16
XATTN_TOKEN_BLOCK = 512
WEIGHT_TILE = 512
KV_COLUMN_BLOCK = 256


def _rmsnorm(x, g):
    ms = jnp.mean(x * x, axis=-1, keepdims=True)
    return x * lax.rsqrt(ms + EPS) * g


def _params(semantics):
    return pltpu.CompilerParams(dimension_semantics=semantics, vmem_limit_bytes=VMEM_LIMIT_BYTES)


def _cast_to_tiles(src, dst):
    tile = dst.shape[2]
    for t in range(dst.shape[0]):
        dst[t] = src[:, t * tile:(t + 1) * tile].astype(BF16)


def _resident(block_shape, index_map):
    return pl.BlockSpec(block_shape, index_map, pipeline_mode=pl.Buffered(1))


def _ffn_kernel(*refs, apply_final_norm, cast_next):
    h_ref, g_ref, wg_ref, wu_ref, wd_ref, fg_ref = refs[:6]
    if cast_next:
        ng_ref, nu_ref, nd_ref, o_ref, ng_out, nu_out, nd_out, u_ref = refs[6:]
    else:
        o_ref, u_ref = refs[6:]
    j = pl.program_id(1)

    @pl.when(j == 0)
    def _():
        x = h_ref[...]
        u_ref[...] = _rmsnorm(x, g_ref[...]).astype(BF16)
        o_ref[...] = x

    u = u_ref[...]
    gate = jnp.dot(u, wg_ref[...], preferred_element_type=F32)
    up = jnp.dot(u, wu_ref[...], preferred_element_type=F32)
    act = (jax.nn.silu(gate) * up * MACARON_WEIGHT).astype(BF16)
    tile = wd_ref.shape[2]
    for t in range(wd_ref.shape[0]):
        cols = slice(t * tile, (t + 1) * tile)
        o_ref[:, cols] += jnp.dot(act, wd_ref[t], preferred_element_type=F32)

    if cast_next:
        ng_out[...] = ng_ref[...].astype(BF16)
        nu_out[...] = nu_ref[...].astype(BF16)
        nd_out[...] = nd_ref[...].astype(BF16)

    if apply_final_norm:

        @pl.when(j == pl.num_programs(1) - 1)
        def _():
            o_ref[...] = _rmsnorm(o_ref[...], fg_ref[...])


def _ffn(h, norms, norm_row, wg, wu, wd, final_g, apply_final_norm, nxt, name):
    m, d = h.shape
    tm, tf = FFN_TOKEN_BLOCK, FFN_HIDDEN_BLOCK
    f = wd.shape[1]
    n_t, tile = wd.shape[0], wd.shape[2]
    n_i = m // tm
    in_specs = [
        pl.BlockSpec((tm, d), lambda i, j: (i, 0)),
        pl.BlockSpec((None, 1, d), lambda i, j: (norm_row, 0, 0)),
        pl.BlockSpec((None, d, tf), lambda i, j: (j, 0, 0)),
        pl.BlockSpec((None, d, tf), lambda i, j: (j, 0, 0)),
        pl.BlockSpec((n_t, tf, tile), lambda i, j: (0, j, 0)),
        pl.BlockSpec((1, d), lambda i, j: (0, 0)),
    ]
    out_specs = [pl.BlockSpec((tm, d), lambda i, j: (i, 0))]
    out_shape = [jax.ShapeDtypeStruct((m, d), F32)]
    args = [h, norms, wg, wu, wd, final_g]
    if nxt is not None:
        n_gate, n_up, n_down, n_layer, n_half = nxt
        ds = d // n_i
        in_specs += [
            pl.BlockSpec((None, None, ds, tf), lambda i, j: (n_layer, n_half, i, j)),
            pl.BlockSpec((None, None, ds, tf), lambda i, j: (n_layer, n_half, i, j)),
            pl.BlockSpec((None, None, tf, ds), lambda i, j: (n_layer, n_half, j, i)),
        ]
        out_specs += [
            pl.BlockSpec((None, ds, tf), lambda i, j: (j, i, 0)),
            pl.BlockSpec((None, ds, tf), lambda i, j: (j, i, 0)),
            pl.BlockSpec((None, tf, ds), lambda i, j: (i // (tile // ds), j, i % (tile // ds))),
        ]
        tiled = jax.ShapeDtypeStruct((f // tf, d, tf), BF16)
        out_shape += [tiled, tiled, jax.ShapeDtypeStruct((n_t, f, tile), BF16)]
        args += [n_gate, n_up, n_down]
    outs = pl.pallas_call(
        functools.partial(_ffn_kernel, apply_final_norm=apply_final_norm, cast_next=nxt is not None),
        grid=(n_i, f // tf),
        in_specs=in_specs,
        out_specs=out_specs,
        out_shape=out_shape,
        scratch_shapes=[pltpu.VMEM((tm, d), BF16)],
        compiler_params=_params(("parallel", "arbitrary")),
        name=name,
    )(*args)
    return outs if nxt is not None else outs[0]


def _pool_kernel(h_ref, g_ref, w_ref, sc_ref, o_ref, buf_ref, tmp0_ref, tmp1_ref, *, blocks_per_seq, halo):
    tm = h_ref.shape[0]
    groups = len(POOL_WINDOWS)
    gd = h_ref.shape[1] // groups
    pad = 2 * halo
    rows = pad + tm
    blk = pl.program_id(0) % blocks_per_seq

    @pl.when(pl.program_id(0) == 0)
    def _():
        buf_ref[0:halo, :] = jnp.zeros((halo, buf_ref.shape[1]), F32)
        tmp0_ref[...] = jnp.zeros(tmp0_ref.shape, F32)
        tmp1_ref[...] = jnp.zeros(tmp1_ref.shape, F32)

    @pl.when(blk == 0)
    def _():
        buf_ref[halo:pad, :] = jnp.zeros((halo, buf_ref.shape[1]), F32)

    x = h_ref[...]
    buf_ref[pad:, :] = _rmsnorm(x, g_ref[...])
    pos = (blk * tm + 1 + lax.broadcasted_iota(jnp.int32, (tm, 1), 0)).astype(F32)

    for g, w in enumerate(POOL_WINDOWS):
        cols = slice(g * gd, (g + 1) * gd)
        src, src_cols = buf_ref, cols
        shift = 1
        for dst in (tmp0_ref, tmp1_ref, tmp0_ref, tmp1_ref):
            if shift >= w:
                break
            dst[halo:, :] = src[halo:rows, src_cols] + src[halo - shift:rows - shift, src_cols]
            src, src_cols = dst, slice(None)
            shift *= 2
        inv_count = 1.0 / jnp.minimum(pos, float(w))
        pooled = (src[pad:, src_cols] * inv_count - buf_ref[pad:, cols]).astype(BF16)
        y = jnp.dot(pooled, w_ref[g], preferred_element_type=F32)
        o_ref[:, cols] = x[:, cols] + y * sc_ref[:, cols]

    buf_ref[halo:pad, :] = buf_ref[tm + halo:tm + pad, :]


def _pool(h, norms, pool_w, pool_scale, layer, j, seq_len):
    m, d = h.shape
    tm = POOL_TOKEN_BLOCK
    halo = max(POOL_WINDOWS)
    groups, gd = pool_w.shape[1], pool_w.shape[2]
    return pl.pallas_call(
        functools.partial(_pool_kernel, blocks_per_seq=seq_len // tm, halo=halo),
        grid=(m // tm,),
        in_specs=[
            pl.BlockSpec((tm, d), lambda i: (i, 0)),
            pl.BlockSpec((None, 1, d), lambda i: (layer, 0, 0)),
            _resident((None, groups, gd, gd), lambda i: (j, 0, 0, 0)),
            pl.BlockSpec((None, 1, d), lambda i: (j, 0, 0)),
        ],
        out_specs=pl.BlockSpec((tm, d), lambda i: (i, 0)),
        out_shape=jax.ShapeDtypeStruct((m, d), F32),
        scratch_shapes=[
            pltpu.VMEM((2 * halo + tm, d), F32),
            pltpu.VMEM((2 * halo + tm, gd), F32),
            pltpu.VMEM((2 * halo + tm, gd), F32),
        ],
        compiler_params=_params(("arbitrary",)),
        name=f"pool_l{layer}",
    )(h, norms, pool_w, pool_scale)


def _softplus(z):
    return jnp.maximum(z, 0.0) + jnp.log1p(jnp.exp(-jnp.abs(z)))


def _sigmoid(z):
    return 0.5 * jnp.tanh(0.5 * z) + 0.5


def _lru_kernel(h_ref, hlag_ref, g_ref, win_ref, cw_ref, cb_ref, wax_ref, ba_ref, bx_ref, lam_ref, wout_ref,
                o_ref, proj0_ref, proj1_ref, a_ref, b_ref, y_ref, carry_ref, *, blocks_per_seq, conv_width):
    tm = h_ref.shape[0]
    r = a_ref.shape[1]
    hd_dim = r // LRU_HEADS
    tail = SUBLANES_F32
    s = pl.program_id(0)
    drain_starts_seq = (s + blocks_per_seq - 1) % blocks_per_seq == 0
    fill_starts_seq = s % blocks_per_seq == 0

    @pl.when(s == 0)
    def _():
        proj0_ref[...] = jnp.zeros(proj0_ref.shape, F32)
        proj1_ref[...] = jnp.zeros(proj1_ref.shape, F32)
        carry_ref[...] = jnp.zeros(carry_ref.shape, F32)

    def step(fill_ref, drain_ref):
        u = _rmsnorm(h_ref[...], g_ref[...]).astype(BF16)
        decay = _softplus(-lam_ref[...])
        rows = lax.broadcasted_iota(jnp.int32, (SUBLANES_F32, r), 0)
        chunk = 2 * r // LRU_PROJ_CHUNKS

        def project(c):
            cols = slice(c * chunk, (c + 1) * chunk)
            fill_ref[tail:, cols] = jnp.dot(u, win_ref[c], preferred_element_type=F32)

        def gates(hd):
            cols = slice(hd * hd_dim, (hd + 1) * hd_dim)
            xcols = slice(r + hd * hd_dim, r + (hd + 1) * hd_dim)
            xc = cb_ref[:, cols]
            for k in range(conv_width):
                back = conv_width - 1 - k
                xc = xc + drain_ref[tail - back:tail - back + tm, xcols] * cw_ref[k:k + 1, cols]
            ra = jnp.dot(xc.astype(BF16), wax_ref[hd], preferred_element_type=F32)
            rg = _sigmoid(ra[:, :hd_dim] + ba_ref[:, cols])
            ig = _sigmoid(ra[:, hd_dim:] + bx_ref[:, cols])
            a = jnp.exp((-LRU_C) * rg * decay[:, cols])
            v = 1.0 - a * a
            a_ref[:, cols] = a
            b_ref[:, cols] = jnp.where(v > 0.0, v * lax.rsqrt(v), 0.0) * ig * xc

        def scan_group(row0, carry):
            a = a_ref[row0:row0 + SUBLANES_F32, :]
            b = b_ref[row0:row0 + SUBLANES_F32, :]
            for dist in (1, 2, 4):
                keep = rows >= dist
                a_prev = jnp.where(keep, pltpu.roll(a, dist, 0), 1.0)
                b_prev = jnp.where(keep, pltpu.roll(b, dist, 0), 0.0)
                b = a * b_prev + b
                a = a * a_prev
            hh = b + a * carry
            gate = drain_ref[tail + row0:tail + row0 + SUBLANES_F32, 0:r]
            y = hh * jax.nn.gelu(gate)
            return y, jnp.broadcast_to(hh[SUBLANES_F32 - 1:, :], (SUBLANES_F32, r))

        state = {"carry": jnp.where(drain_starts_seq, 0.0, carry_ref[...])}

        def scan_pair(row0):
            y0, carry = scan_group(row0, state["carry"])
            y1, carry = scan_group(row0 + SUBLANES_F32, carry)
            y_ref[row0:row0 + SUBLANES_BF16, :] = jnp.concatenate([y0, y1], axis=0).astype(BF16)
            state["carry"] = carry

        vector_tasks = [functools.partial(gates, hd) for hd in range(LRU_HEADS)]
        vector_tasks += [functools.partial(scan_pair, row0) for row0 in range(0, tm, SUBLANES_BF16)]
        per_chunk = -(-len(vector_tasks) // LRU_PROJ_CHUNKS)
        for c in range(LRU_PROJ_CHUNKS):
            project(c)
            for task in vector_tasks[c * per_chunk:(c + 1) * per_chunk]:
                task()

        fill_ref[0:tail, r:] = jnp.where(fill_starts_seq, 0.0, drain_ref[tm:tm + tail, r:])
        carry_ref[...] = state["carry"]
        out_tile = wout_ref.shape[2]
        for t in range(wout_ref.shape[0]):
            cols = slice(t * out_tile, (t + 1) * out_tile)
            o_ref[:, cols] = hlag_ref[:, cols] + jnp.dot(y_ref[...], wout_ref[t], preferred_element_type=F32)

    @pl.when(s % 2 == 0)
    def _():
        step(proj0_ref, proj1_ref)

    @pl.when(s % 2 == 1)
    def _():
        step(proj1_ref, proj0_ref)


def _lru(h, norms, w_in, conv_w, conv_b, w_ax, b_a, b_x, lam, w_out, layer, j, seq_len):
    m, d = h.shape
    tm = LRU_TOKEN_BLOCK
    n_blocks = m // tm
    r = w_out.shape[1]
    conv_width = conv_w.shape[1]
    heads, hd_dim = w_ax.shape[1], w_ax.shape[2]
    vec = lambda idx: pl.BlockSpec((None, 1, r), lambda s: (idx, 0, 0))
    lagged = lambda s: (jnp.maximum(s - 1, 0), 0)
    proj = pltpu.VMEM((SUBLANES_F32 + tm, 2 * r), F32)
    return pl.pallas_call(
        functools.partial(_lru_kernel, blocks_per_seq=seq_len // tm, conv_width=conv_width),
        grid=(n_blocks + 1,),
        in_specs=[
            pl.BlockSpec((tm, d), lambda s: (jnp.minimum(s, n_blocks - 1), 0)),
            pl.BlockSpec((tm, d), lagged),
            pl.BlockSpec((None, 1, d), lambda s: (layer, 0, 0)),
            _resident((LRU_PROJ_CHUNKS, d, 2 * r // LRU_PROJ_CHUNKS), lambda s: (0, 0, 0)),
            pl.BlockSpec((None, conv_width, r), lambda s: (j, 0, 0)),
            vec(j),
            _resident((None, heads, hd_dim, 2 * hd_dim), lambda s: (j, 0, 0, 0)),
            vec(j),
            vec(j),
            vec(j),
            _resident(w_out.shape, lambda s: (0, 0, 0)),
        ],
        out_specs=pl.BlockSpec((tm, d), lagged),
        out_shape=jax.ShapeDtypeStruct((m, d), F32),
        scratch_shapes=[
            proj,
            proj,
            pltpu.VMEM((tm, r), F32),
            pltpu.VMEM((tm, r), F32),
            pltpu.VMEM((tm, r), BF16),
            pltpu.VMEM((SUBLANES_F32, r), F32),
        ],
        compiler_params=_params(("arbitrary",)),
        name=f"lru_l{layer}",
    )(h, h, norms, w_in, conv_w, conv_b, w_ax, b_a, b_x, lam, w_out)


def _kv_kernel(m_ref, g_ref, wk_ref, wv_ref, fg_ref, fu_ref, fd_ref, wq_ref, wo_ref,
               k_out, v_out, fg_out, fu_out, fd_out, wq_out, wo_out, mn_ref):
    first = jnp.logical_and(pl.program_id(0) == 0, pl.program_id(1) == 0)

    @pl.when(first)
    def _():
        mn_ref[...] = _rmsnorm(m_ref[...], g_ref[...]).astype(BF16)

    mn = mn_ref[...]
    k_out[...] = jnp.dot(mn, wk_ref[...].astype(BF16), preferred_element_type=F32).astype(BF16)
    v_out[...] = jnp.dot(mn, wv_ref[...].astype(BF16), preferred_element_type=F32).astype(BF16)

    _cast_to_tiles(fg_ref, fg_out)
    _cast_to_tiles(fu_ref, fu_out)
    _cast_to_tiles(fd_ref, fd_out)
    _cast_to_tiles(wq_ref, wq_out)
    _cast_to_tiles(wo_ref, wo_out)


def _kv(mem2d, mem_norm, w_k, w_v, w_gate, w_up, w_down, w_q, w_o):
    rows, d = mem2d.shape
    depth = w_k.shape[0]
    f = w_down.shape[2]
    tn, tf = KV_COLUMN_BLOCK, FFN_HIDDEN_BLOCK
    n_n = d // tn
    steps = depth * n_n
    step = lambda l, n: l * n_n + n
    kv_out = jax.ShapeDtypeStruct((depth, rows, d), BF16)
    outs = pl.pallas_call(
        _kv_kernel,
        grid=(depth, n_n),
        in_specs=[
            _resident((rows, d), lambda l, n: (0, 0)),
            pl.BlockSpec((1, d), lambda l, n: (0, 0)),
            pl.BlockSpec((None, d, tn), lambda l, n: (l, 0, n)),
            pl.BlockSpec((None, d, tn), lambda l, n: (l, 0, n)),
            pl.BlockSpec((None, None, d // steps, f), lambda l, n: (0, 0, step(l, n), 0)),
            pl.BlockSpec((None, None, d // steps, f), lambda l, n: (0, 0, step(l, n), 0)),
            pl.BlockSpec((None, None, f // steps, d), lambda l, n: (0, 0, step(l, n), 0)),
            pl.BlockSpec((None, d // steps, d), lambda l, n: (0, step(l, n), 0)),
            pl.BlockSpec((None, d // steps, d), lambda l, n: (0, step(l, n), 0)),
        ],
        out_specs=[
            pl.BlockSpec((None, rows, tn), lambda l, n: (l, 0, n)),
            pl.BlockSpec((None, rows, tn), lambda l, n: (l, 0, n)),
            pl.BlockSpec((f // tf, d // steps, tf), lambda l, n: (0, step(l, n), 0)),
            pl.BlockSpec((f // tf, d // steps, tf), lambda l, n: (0, step(l, n), 0)),
            pl.BlockSpec((d // WEIGHT_TILE, f // steps, WEIGHT_TILE), lambda l, n: (0, step(l, n), 0)),
            pl.BlockSpec((d // WEIGHT_TILE, d // steps, WEIGHT_TILE), lambda l, n: (0, step(l, n), 0)),
            pl.BlockSpec((d // WEIGHT_TILE, d // steps, WEIGHT_TILE), lambda l, n: (0, step(l, n), 0)),
        ],
        out_shape=[kv_out, kv_out,
                   jax.ShapeDtypeStruct((f // tf, d, tf), BF16), jax.ShapeDtypeStruct((f // tf, d, tf), BF16),
                   jax.ShapeDtypeStruct((d // WEIGHT_TILE, f, WEIGHT_TILE), BF16),
                   jax.ShapeDtypeStruct((d // WEIGHT_TILE, d, WEIGHT_TILE), BF16),
                   jax.ShapeDtypeStruct((d // WEIGHT_TILE, d, WEIGHT_TILE), BF16)],
        scratch_shapes=[pltpu.VMEM((rows, d), BF16)],
        compiler_params=_params(("arbitrary", "arbitrary")),
        name="kv_proj",
    )(mem2d, mem_norm, w_k, w_v, w_gate, w_up, w_down, w_q, w_o)
    return outs[0], outs[1], tuple(outs[2:5]), tuple(outs[5:7])


def _xattn_kernel(*refs, n_casts):
    h_ref, g_ref, wq_ref, k_ref, v_ref, wo_ref = refs[:6]
    cast_in = refs[6:6 + n_casts]
    o_ref = refs[6 + n_casts]
    cast_out = refs[7 + n_casts:7 + 2 * n_casts]
    q_ref, a_ref = refs[7 + 2 * n_casts:]
    d = h_ref.shape[1]
    hd_dim = d // XATTN_HEADS
    scale = hd_dim ** -0.5
    x = h_ref[...]
    u = _rmsnorm(x, g_ref[...]).astype(BF16)
    tile = wq_ref.shape[2]
    for t in range(wq_ref.shape[0]):
        q_ref[:, t * tile:(t + 1) * tile] = jnp.dot(u, wq_ref[t], preferred_element_type=F32).astype(BF16)
    head_cols = [slice(hd * hd_dim, (hd + 1) * hd_dim) for hd in range(XATTN_HEADS)]
    scores = [lax.dot_general(q_ref[:, cols], k_ref[:, cols], (((1,), (1,)), ((), ())),
                              preferred_element_type=F32) * scale for cols in head_cols]
    probs = []
    for s in scores:
        e = jnp.exp(s - jnp.max(s, axis=-1, keepdims=True))
        probs.append((e * (1.0 / jnp.sum(e, axis=-1, keepdims=True))).astype(BF16))
    for cols, p in zip(head_cols, probs):
        a_ref[:, cols] = jnp.dot(p, v_ref[:, cols], preferred_element_type=F32).astype(BF16)
    for t in range(wo_ref.shape[0]):
        cols = slice(t * tile, (t + 1) * tile)
        o_ref[:, cols] = x[:, cols] + jnp.dot(a_ref[...], wo_ref[t], preferred_element_type=F32)
    for src, dst in zip(cast_in, cast_out):
        _cast_to_tiles(src, dst)


def _xattn(h, norm, w_q, k_all, v_all, w_o, layer, seq_len, mem_len, casts):
    m, d = h.shape
    tm = XATTN_TOKEN_BLOCK
    n_blocks = m // tm
    blocks_per_seq = seq_len // tm
    in_specs = [
        pl.BlockSpec((tm, d), lambda i: (i, 0)),
        pl.BlockSpec((None, 1, d), lambda i: (layer, 0, 0)),
        _resident(w_q.shape, lambda i: (0, 0, 0)),
        pl.BlockSpec((None, mem_len, d), lambda i: (layer, i // blocks_per_seq, 0)),
        pl.BlockSpec((None, mem_len, d), lambda i: (layer, i // blocks_per_seq, 0)),
        _resident(w_o.shape, lambda i: (0, 0, 0)),
    ]
    out_specs = [pl.BlockSpec((tm, d), lambda i: (i, 0))]
    out_shape = [jax.ShapeDtypeStruct((m, d), F32)]
    args = [h, norm, w_q, k_all, v_all, w_o]
    for stacked, idx, tile in casts:
        _, rows, cols = stacked.shape
        slab = rows // n_blocks
        in_specs.append(pl.BlockSpec((None, slab, cols), lambda i, idx=idx: (idx, i, 0)))
        out_specs.append(pl.BlockSpec((cols // tile, slab, tile), lambda i: (0, i, 0)))
        out_shape.append(jax.ShapeDtypeStruct((cols // tile, rows, tile), BF16))
        args.append(stacked)
    outs = pl.pallas_call(
        functools.partial(_xattn_kernel, n_casts=len(casts)),
        grid=(n_blocks,),
        in_specs=in_specs,
        out_specs=out_specs,
        out_shape=out_shape,
        scratch_shapes=[pltpu.VMEM((tm, d), BF16), pltpu.VMEM((tm, d), BF16)],
        compiler_params=_params(("arbitrary",)),
        name=f"xattn_l{layer}",
    )(*args)
    return outs[0], list(outs[1:])


def kernel(x, mem, ffn_norm, w_ffn_gate, w_ffn_up, w_ffn_down, mix_norm, pool_w, pool_scale, lru_w_in, lru_conv_w, lru_conv_b, lru_w_a, lru_b_a, lru_w_x, lru_b_x, lru_lambda, lru_w_out, xattn_norm, mem_norm, w_q, w_k, w_v, w_o, final_norm):
    batch, seq_len, d = x.shape
    mem_len = mem.shape[1]
    depth = ffn_norm.shape[0]
    n_mixers = 2

    row = lambda a: a.reshape(-1, 1, a.shape[-1])
    ffn_norm_r, mix_norm_r, xattn_norm_r = row(ffn_norm), row(mix_norm), row(xattn_norm)
    pool_scale_r = row(pool_scale)
    conv_b_r, b_a_r, b_x_r, lam_r = row(lru_conv_b), row(lru_b_a), row(lru_b_x), row(lru_lambda)
    final_g = final_norm.reshape(1, d)

    pool_w_b = pool_w.astype(BF16)
    w_ax = jnp.concatenate([lru_w_a, lru_w_x], axis=-1).astype(BF16)
    lru_w = None

    k_all, v_all, ffn_w, attn_w = _kv(mem.reshape(batch * mem_len, d), mem_norm.reshape(1, d), w_k, w_v,
                                      w_ffn_gate, w_ffn_up, w_ffn_down, w_q, w_o)

    def ffn(h, ffn_w, layer, half):
        last = layer == depth - 1 and half == 1
        nxt = None if last else (w_ffn_gate, w_ffn_up, w_ffn_down) + ((layer, 1) if half == 0 else (layer + 1, 0))
        outs = _ffn(h, ffn_norm_r, layer * 2 + half, *ffn_w, final_g, last, nxt, f"ffn_l{layer}_{half}")
        return (outs, None) if last else (outs[0], tuple(outs[1:]))

    h = x.reshape(batch * seq_len, d)
    for i in range(depth):
        j = i // n_mixers
        h, ffn_w = ffn(h, ffn_w, i, 0)
        if i % n_mixers == 0:
            h = _pool(h, mix_norm_r, pool_w_b, pool_scale_r, i, j, seq_len)
        else:
            h = _lru(h, mix_norm_r, lru_w[0], lru_conv_w, conv_b_r, w_ax, b_a_r, b_x_r, lam_r, lru_w[1],
                     i, j, seq_len)
        casts = []
        if i + 1 < depth:
            casts += [(w_q, i + 1, WEIGHT_TILE), (w_o, i + 1, WEIGHT_TILE)]
            if (i + 1) % n_mixers == 1:
                proj_tile = lru_w_in.shape[2] // LRU_PROJ_CHUNKS
                casts += [(lru_w_in, (i + 1) // n_mixers, proj_tile), (lru_w_out, (i + 1) // n_mixers, WEIGHT_TILE)]
        h, cast = _xattn(h, xattn_norm_r, attn_w[0], k_all, v_all, attn_w[1], i, seq_len, mem_len, casts)
        if i + 1 < depth:
            attn_w = tuple(cast[:2])
            if (i + 1) % n_mixers == 1:
                lru_w = tuple(cast[2:])
        h, ffn_w = ffn(h, ffn_w, i, 1)
    return h.reshape(batch, seq_len, d)
```
